```python
import jax, jax.numpy as jnp
from jax import lax
import numpy as np

D_MODEL = 1024
BATCH = 8
SEQ = 4096
DEPTH = 2

BLOCK = 128
A_HEADS = 8
A_HEAD_DIM = 64
A_WIDTH = A_HEADS * A_HEAD_DIM
DILATED = ((128, 1), (512, 4), (2048, 16))
B_WIDTH = D_MODEL - A_WIDTH
SCONV_WIDTH = 3
IN_SPLITS = (A_WIDTH, A_WIDTH, A_WIDTH, B_WIDTH, B_WIDTH, B_WIDTH)
IN_WIDTH = sum(IN_SPLITS)
RET_HEADS = 4
RET_KDIM = D_MODEL // RET_HEADS
RET_VDIM = 2 * D_MODEL // RET_HEADS
RET_CHUNK = 128
ROPE_BASE = 10000.0
D_FF = 2816
FFN_CONV_WIDTH = 3
EPS = 1e-6
N_EVEN = (DEPTH + 1) // 2
N_ODD = DEPTH // 2

kernel_name = "hybrid_dilated_shortconv_retention_trunk"


def rms_norm(x, g):
    xf = x.astype(jnp.float32)
    y = xf * lax.rsqrt(jnp.mean(xf * xf, axis=-1, keepdims=True) + EPS)
    return (y * g.astype(jnp.float32)).astype(x.dtype)


def causal_dwconv(x, w):
    K = w.shape[0]
    S = x.shape[1]
    xp = jnp.pad(x, ((0, 0), (K - 1, 0), (0, 0)))
    y = xp[:, 0:S] * w[0]
    for j in range(1, K):
        y = y + xp[:, j:j + S] * w[j]
    return y


def banded_causal_attention(q, k, v, reach):
    L, hd = q.shape[-2], q.shape[-1]
    lead = q.shape[:-2]
    nb = -(-L // BLOCK)
    pad_end = nb * BLOCK - L
    zp = [(0, 0)] * len(lead)
    qb = jnp.pad(q, zp + [(0, pad_end), (0, 0)]).reshape(*lead, nb, BLOCK, hd)
    kp = jnp.pad(k, zp + [(BLOCK, pad_end), (0, 0)]).reshape(*lead, nb + 1, BLOCK, hd)
    vp = jnp.pad(v, zp + [(BLOCK, pad_end), (0, 0)]).reshape(*lead, nb + 1, BLOCK, hd)
    kb = jnp.concatenate([kp[..., :-1, :, :], kp[..., 1:, :, :]], axis=-2)
    vb = jnp.concatenate([vp[..., :-1, :, :], vp[..., 1:, :, :]], axis=-2)
    s = jnp.einsum('...nqd,...nkd->...nqk', qb.astype(jnp.float32), kb.astype(jnp.float32)) * (hd ** -0.5)
    blk = jnp.arange(nb)[:, None, None]
    qi = jnp.arange(BLOCK)[None, :, None]
    kj = jnp.arange(2 * BLOCK)[None, None, :]
    rel = BLOCK + qi - kj
    kpos = blk * BLOCK - BLOCK + kj
    mask = (rel >= 0) & (rel <= reach) & (kpos >= 0)
    s = jnp.where(mask, s, -jnp.inf)
    m = jnp.max(s, axis=-1, keepdims=True)
    p = jnp.exp(s - m)
    den = jnp.sum(p, axis=-1, keepdims=True)
    o = jnp.einsum('...nqk,...nkd->...nqd', p, vb.astype(jnp.float32)) / den
    lse = (m + jnp.log(den))[..., 0]
    o = o.reshape(*lead, nb * BLOCK, hd)[..., :L, :]
    lse = lse.reshape(*lead, nb * BLOCK)[..., :L]
    return o, lse


def dilated_attention(q, k, v):
    Bn, H, S, hd = q.shape
    outs, lses = [], []
    for window, dil in DILATED:
        L = S // dil
        split = lambda t: t.reshape(Bn, H, L, dil, hd).swapaxes(2, 3)
        o, lse = banded_causal_attention(split(q), split(k), split(v), window // dil)
        outs.append(o.swapaxes(2, 3).reshape(Bn, H, S, hd))
        lses.append(lse.swapaxes(2, 3).reshape(Bn, H, S))
    wts = jax.nn.softmax(jnp.stack(lses), axis=0)
    return jnp.sum(wts[..., None] * jnp.stack(outs), axis=0)


def even_mixer(xn, w_in, q_gain, k_gain, sconv_w, w_out):
    Bn, S, _ = xn.shape
    proj = xn @ w_in
    q, k, v, gate_b, gate_c, h = jnp.split(proj, np.cumsum(IN_SPLITS)[:-1].tolist(), axis=-1)
    heads = lambda t: t.reshape(Bn, S, A_HEADS, A_HEAD_DIM)
    q = rms_norm(heads(q), q_gain).transpose(0, 2, 1, 3)
    k = rms_norm(heads(k), k_gain).transpose(0, 2, 1, 3)
    v = heads(v).transpose(0, 2, 1, 3)
    a = dilated_attention(q, k, v).astype(xn.dtype).transpose(0, 2, 1, 3).reshape(Bn, S, A_WIDTH)
    b = gate_b * causal_dwconv(gate_c * h, sconv_w)
    return jnp.concatenate([a, b], axis=-1) @ w_out


def rotary(x, pos):
    half = x.shape[-1] // 2
    inv = ROPE_BASE ** (-jnp.arange(half, dtype=jnp.float32) / half)
    ang = pos[:, None] * inv[None, :]
    cos = jnp.cos(ang)[None, :, None, :]
    sin = jnp.sin(ang)[None, :, None, :]
    x1, x2 = x[..., :half], x[..., half:]
    return jnp.concatenate([x1 * cos - x2 * sin, x1 * sin + x2 * cos], axis=-1)


def chunkwise_retention(q, k, v):
    Bn, S, H, dk = q.shape
    dv = v.shape[-1]
    nc = S // RET_CHUNK
    log_g = jnp.log1p(-(2.0 ** (-5.0 - jnp.arange(H, dtype=jnp.float32))))
    i = jnp.arange(RET_CHUNK, dtype=jnp.float32)
    diff = i[:, None] - i[None, :]
    inner_decay = jnp.where(diff >= 0, jnp.exp(log_g[:, None, None] * jnp.maximum(diff, 0.0)), 0.0)
    q_decay = jnp.exp(log_g[:, None] * (i + 1.0))
    k_decay = jnp.exp(log_g[:, None] * (RET_CHUNK - 1.0 - i))
    chunk_decay = jnp.exp(log_g * RET_CHUNK)
    to_chunks = lambda t: t.reshape(Bn, nc, RET_CHUNK, H, t.shape[-1]).transpose(1, 0, 3, 2, 4)

    def step(state, qkv):
        qc, kc, vc = qkv
        scores = jnp.einsum('bhqd,bhkd->bhqk', qc, kc) * inner_decay
        o = (jnp.einsum('bhqk,bhke->bhqe', scores, vc)
             + jnp.einsum('bhqd,bhde->bhqe', qc, state) * q_decay[..., None])
        state = (state * chunk_decay[:, None, None]
                 + jnp.einsum('bhkd,bhke->bhde', kc * k_decay[..., None], vc))
        return state, o

    state0 = jnp.zeros((Bn, H, dk, dv), jnp.float32)
    _, o = lax.scan(step, state0, (to_chunks(q), to_chunks(k), to_chunks(v)))
    return o.transpose(1, 0, 3, 2, 4).reshape(Bn, S, H, dv)


def retention_mixer(xn, wq, wk, wv, wg, gn_gain, wo):
    Bn, S, _ = xn.shape
    pos = jnp.arange(S, dtype=jnp.float32)
    q = (xn @ wq).reshape(Bn, S, RET_HEADS, RET_KDIM).astype(jnp.float32)
    k = (xn @ wk).reshape(Bn, S, RET_HEADS, RET_KDIM).astype(jnp.float32) * (RET_KDIM ** -0.5)
    v = (xn @ wv).reshape(Bn, S, RET_HEADS, RET_VDIM).astype(jnp.float32)
    o = chunkwise_retention(rotary(q, pos), rotary(k, pos), v)
    mu = jnp.mean(o, axis=-1, keepdims=True)
    var = jnp.mean(jnp.square(o - mu), axis=-1, keepdims=True)
    o = (o - mu) * lax.rsqrt(var + EPS) * gn_gain.astype(jnp.float32)
    o = o.reshape(Bn, S, RET_HEADS * RET_VDIM).astype(xn.dtype)
    return (jax.nn.silu(xn @ wg) * o) @ wo


def conv_glu_ffn(xn, w_up, conv_w, w_down):
    u = causal_dwconv(xn @ w_up, conv_w)
    g, val = jnp.split(u, 2, axis=-1)
    return (jax.nn.silu(g) * val) @ w_down


def setup_inputs(seed: int = 0) -> dict:
    key = jax.random.key(seed)
    ks = jax.random.split(key, 24)
    nrm = lambda k, shape, s: jax.random.normal(k, shape, jnp.float32) * s
    gain = lambda k, shape: 1.0 + 0.02 * jax.random.normal(k, shape, jnp.float32)
    D = D_MODEL
    return {
        "x": jax.random.normal(ks[0], (BATCH, SEQ, D), jnp.float32),
        "even_norm": gain(ks[1], (N_EVEN, D)),
        "even_w_in": nrm(ks[2], (N_EVEN, D, IN_WIDTH), D ** -0.5),
        "even_q_gain": gain(ks[3], (N_EVEN, A_HEAD_DIM)),
        "even_k_gain": gain(ks[4], (N_EVEN, A_HEAD_DIM)),
        "even_sconv_w": nrm(ks[5], (N_EVEN, SCONV_WIDTH, B_WIDTH), SCONV_WIDTH ** -0.5),
        "even_w_out": nrm(ks[6], (N_EVEN, A_WIDTH + B_WIDTH, D), (A_WIDTH + B_WIDTH) ** -0.5),
        "odd_norm": gain(ks[7], (N_ODD, D)),
        "ret_wq": nrm(ks[8], (N_ODD, D, RET_HEADS * RET_KDIM), D ** -0.5),
        "ret_wk": nrm(ks[9], (N_ODD, D, RET_HEADS * RET_KDIM), D ** -0.5),
        "ret_wv": nrm(ks[10], (N_ODD, D, RET_HEADS * RET_VDIM), D ** -0.5),
        "ret_wg": nrm(ks[11], (N_ODD, D, RET_HEADS * RET_VDIM), D ** -0.5),
        "ret_gn_gain": gain(ks[12], (N_ODD, RET_HEADS, RET_VDIM)),
        "ret_wo": nrm(ks[13], (N_ODD, RET_HEADS * RET_VDIM, D), (RET_HEADS * RET_VDIM) ** -0.5),
        "ffn_norm": gain(ks[14], (DEPTH, D)),
        "ffn_w_up": nrm(ks[15], (DEPTH, D, 2 * D_FF), D ** -0.5),
        "ffn_conv_w": nrm(ks[16], (DEPTH, FFN_CONV_WIDTH, 2 * D_FF), FFN_CONV_WIDTH ** -0.5),
        "ffn_w_down": nrm(ks[17], (DEPTH, D_FF, D), D_FF ** -0.5),
    }


def reference(x, even_norm, even_w_in, even_q_gain, even_k_gain, even_sconv_w, even_w_out,
              odd_norm, ret_wq, ret_wk, ret_wv, ret_wg, ret_gn_gain, ret_wo,
              ffn_norm, ffn_w_up, ffn_conv_w, ffn_w_down):
    for l in range(DEPTH):
        if l % 2 == 0:
            e = l // 2
            x = x + even_mixer(rms_norm(x, even_norm[e]), even_w_in[e], even_q_gain[e],
                               even_k_gain[e], even_sconv_w[e], even_w_out[e])
        else:
            o = l // 2
            x = x + retention_mixer(rms_norm(x, odd_norm[o]), ret_wq[o], ret_wk[o], ret_wv[o],
                                    ret_wg[o], ret_gn_gain[o], ret_wo[o])
        x = x + conv_glu_ffn(rms_norm(x, ffn_norm[l]), ffn_w_up[l], ffn_conv_w[l], ffn_w_down[l])
    return x
```

```python
import functools
import math

import jax
import jax.numpy as jnp
from jax import lax
from jax.experimental import pallas as pl
from jax.experimental.pallas import tpu as pltpu

F32 = jnp.float32
BF16 = jnp.bfloat16

D_MODEL = 1024
A_HEADS = 8
A_HEAD_DIM = 64
A_WIDTH = A_HEADS * A_HEAD_DIM
B_WIDTH = D_MODEL - A_WIDTH
DILATIONS = (1, 4, 16)
REACH = 128
RET_HEADS = 4
RET_KDIM = D_MODEL // RET_HEADS
RET_VDIM = 2 * D_MODEL // RET_HEADS
RET_CHUNK = 128
ROPE_BASE = 10000.0
D_FF = 2816
EPS = 1e-6

SUBLANES = 8
LANES = 128
VMEM_LIMIT_BYTES = 56 * 1024 * 1024

ROW_TILE = 512
FF_TILE = 256
ATTN_Q_TILE = 256
RET_TILE = 256


def _params(*semantics):
    return pltpu.CompilerParams(dimension_semantics=semantics, vmem_limit_bytes=VMEM_LIMIT_BYTES)


def _resident(shape):
    nd = len(shape)
    return pl.BlockSpec(shape, lambda *_: (0,) * nd, pipeline_mode=pl.Buffered(1))


def _rows(tm, width):
    return pl.BlockSpec((None, tm, width), lambda b, i: (b, i, 0))


def _rms(x, g):
    ms = jnp.mean(x * x, axis=-1, keepdims=True)
    return x * lax.rsqrt(ms + EPS) * g


def _silu(g):
    return g / (1.0 + jnp.exp(-g))


def _shift_rows(cur, prev, k):
    rolled = pltpu.roll(cur, k, axis=0)
    prev_rolled = pltpu.roll(prev, k, axis=0)
    row = lax.broadcasted_iota(jnp.int32, prev.shape, 0)
    head = jnp.where(row < k, prev_rolled, rolled[:SUBLANES])
    return jnp.concatenate([head, rolled[SUBLANES:]], axis=0)


def _causal_conv3(cur, prev, w):
    return (_shift_rows(cur, prev, 2) * w[0:1] + _shift_rows(cur, prev, 1) * w[1:2]
            + cur * w[2:3])


def _inproj_kernel(x_ref, g_ref, w_ref, qg_ref, kg_ref, cw_ref, hsum_ref,
                   q_ref, k_ref, v_ref, b_ref, halo_ref):
    @pl.when(pl.program_id(1) == 0)
    def _():
        halo_ref[...] = jnp.zeros_like(halo_ref)

    xn = _rms(x_ref[...], g_ref[...]).astype(BF16)

    def proj(idx):
        return jnp.dot(xn, w_ref[:, idx * A_WIDTH:(idx + 1) * A_WIDTH], preferred_element_type=F32)

    def head_norm(t, gain):
        t2 = (t * t).astype(BF16)
        half = A_WIDTH // 2
        ss = jnp.concatenate(
            [jnp.dot(t2[:, :half], hsum_ref[...], preferred_element_type=F32),
             jnp.dot(t2[:, half:], hsum_ref[...], preferred_element_type=F32)], axis=1)
        return t * lax.rsqrt(ss * (1.0 / A_HEAD_DIM) + EPS) * gain

    q_ref[...] = head_norm(proj(0), qg_ref[...]).astype(BF16)
    k_ref[...] = head_norm(proj(1), kg_ref[...]).astype(BF16)
    v_ref[...] = proj(2).astype(BF16)
    gate_b = proj(3)
    s = proj(4) * proj(5)
    conv = _causal_conv3(s, halo_ref[...], cw_ref[...])
    halo_ref[...] = s[s.shape[0] - SUBLANES:]
    b_ref[...] = (gate_b * conv).astype(BF16)


def _inproj(x, g, w_in, q_gain, k_gain, sconv_w, hsum):
    B, S, D = x.shape
    tm = ROW_TILE
    out = jax.ShapeDtypeStruct((B, S, A_WIDTH), BF16)
    return pl.pallas_call(
        _inproj_kernel,
        grid=(B, S // tm),
        in_specs=[_rows(tm, D), _resident((1, D)), _resident(w_in.shape), _resident((1, A_WIDTH)),
                  _resident((1, A_WIDTH)), _resident(sconv_w.shape), _resident(hsum.shape)],
        out_specs=[_rows(tm, A_WIDTH)] * 4,
        out_shape=[out] * 4,
        scratch_shapes=[pltpu.VMEM((SUBLANES, B_WIDTH), F32)],
        compiler_params=_params("parallel", "arbitrary"),
        name="even_inproj",
    )(x, g, w_in, q_gain, k_gain, sconv_w, hsum)


def _attn_kernel(q_ref, kp_ref, kc_ref, vp_ref, vc_ref, o_ref, lse_ref):
    first_tile = pl.program_id(2) == 0
    qi = lax.broadcasted_iota(jnp.int32, (REACH, 2 * REACH), 0)
    kj = lax.broadcasted_iota(jnp.int32, (REACH, 2 * REACH), 1)
    band = (kj >= qi) & (kj <= qi + REACH)
    band0 = band & ((kj >= REACH) | jnp.logical_not(first_tile))
    lane = lax.broadcasted_iota(jnp.int32, (REACH, LANES), 1)
    scale = A_HEAD_DIM ** -0.5

    for j in range(q_ref.shape[0] // REACH):
        rows = slice(j * REACH, (j + 1) * REACH)
        lse_tile = jnp.zeros((REACH, LANES), F32)
        for h in range(A_HEADS):
            cs = slice(h * A_HEAD_DIM, (h + 1) * A_HEAD_DIM)
            qh = q_ref[rows, cs]
            if j == 0:
                kh = jnp.concatenate([kp_ref[:, cs], kc_ref[0:REACH, cs]], axis=0)
                vh = jnp.concatenate([vp_ref[:, cs], vc_ref[0:REACH, cs]], axis=0)
                mask = band0
            else:
                kh = kc_ref[(j - 1) * REACH:(j + 1) * REACH, cs]
                vh = vc_ref[(j - 1) * REACH:(j + 1) * REACH, cs]
                mask = band
            s = lax.dot_general(qh, kh, (((1,), (1,)), ((), ())), preferred_element_type=F32) * scale
            s = jnp.where(mask, s, -1e30)
            m = jnp.max(s, axis=-1, keepdims=True)
            p = jnp.exp(s - m)
            den = jnp.sum(p, axis=-1, keepdims=True)
            o = jnp.dot(p.astype(BF16), vh, preferred_element_type=F32) / den
            o_ref[rows, cs] = o.astype(o_ref.dtype)
            lse_tile = jnp.where(lane == h, m + jnp.log(den), lse_tile)
        lse_ref[rows, :] = lse_tile


def _dilated_attention_one(q, k, v, dil):
    B, S, W = q.shape
    L = S // dil
    tq = min(ATTN_Q_TILE, L)
    view = lambda t: t.reshape(B, L, dil * W)
    cur = pl.BlockSpec((None, tq, W), lambda b, r, i: (b, i, r))
    prev = pl.BlockSpec((None, REACH, W),
                        lambda b, r, i: (b, jnp.maximum(i * (tq // REACH) - 1, 0), r))
    o, lse = pl.pallas_call(
        _attn_kernel,
        grid=(B, dil, L // tq),
        in_specs=[cur, prev, cur, prev, cur],
        out_specs=[cur, pl.BlockSpec((None, tq, LANES), lambda b, r, i: (b, i, r))],
        out_shape=[jax.ShapeDtypeStruct((B, L, dil * W), BF16),
                   jax.ShapeDtypeStruct((B, L, dil * LANES), F32)],
        compiler_params=_params("parallel", "parallel", "arbitrary"),
        name=f"dilated_attn_d{dil}",
    )(view(q), view(k), view(k), view(v), view(v))
    return o.reshape(B, S, W), lse.reshape(B, S, LANES)


def _outproj_kernel(x_ref, o1_ref, o2_ref, o3_ref, l1_ref, l2_ref, l3_ref, b_ref, expand_ref,
                    w_ref, out_ref):
    ls = [l1_ref[...], l2_ref[...], l3_ref[...]]
    mx = jnp.maximum(jnp.maximum(ls[0], ls[1]), ls[2])
    es = [jnp.exp(l - mx) for l in ls]
    tot = es[0] + es[1] + es[2]
    a = None
    for e, o_ref in zip(es, (o1_ref, o2_ref, o3_ref)):
        w = e / tot
        hi = w.astype(BF16)
        lo = (w - hi.astype(F32)).astype(BF16)
        wf = (jnp.dot(hi, expand_ref[...], preferred_element_type=F32)
              + jnp.dot(lo, expand_ref[...], preferred_element_type=F32))
        term = wf * o_ref[...].astype(F32)
        a = term if a is None else a + term
    y = (jnp.dot(a.astype(BF16), w_ref[:A_WIDTH, :], preferred_element_type=F32)
         + jnp.dot(b_ref[...], w_ref[A_WIDTH:, :], preferred_element_type=F32))
    out_ref[...] = x_ref[...] + y


def _outproj(x, os, lses, b, expand, w_out):
    B, S, D = x.shape
    tm = ROW_TILE
    return pl.pallas_call(
        _outproj_kernel,
        grid=(B, S // tm),
        in_specs=[_rows(tm, D)] + [_rows(tm, A_WIDTH)] * 3 + [_rows(tm, LANES)] * 3
                 + [_rows(tm, B_WIDTH), _resident(expand.shape), _resident(w_out.shape)],
        out_specs=_rows(tm, D),
        out_shape=jax.ShapeDtypeStruct((B, S, D), F32),
        compiler_params=_params("parallel", "parallel"),
        name="even_outproj",
    )(x, *os, *lses, b, expand, w_out)


def _ffn_kernel(x_ref, g_ref, wup_ref, cw_ref, wdn_ref, o_ref, h_ref, halo_ref):
    @pl.when(pl.program_id(1) == 0)
    def _():
        halo_ref[...] = jnp.zeros_like(halo_ref)

    x = x_ref[...]
    tm = x.shape[0]
    xn = _rms(x, g_ref[...]).astype(BF16)

    def conv_up(col):
        cols = slice(col, col + FF_TILE)
        u = jnp.dot(xn, wup_ref[:, cols], preferred_element_type=F32)
        y = _causal_conv3(u, halo_ref[:, cols], cw_ref[:, cols])
        halo_ref[:, cols] = u[tm - SUBLANES:]
        return y

    for c in range(D_FF // FF_TILE):
        gate = conv_up(c * FF_TILE)
        val = conv_up(D_FF + c * FF_TILE)
        h_ref[:, c * FF_TILE:(c + 1) * FF_TILE] = (_silu(gate) * val).astype(BF16)
    o_ref[...] = x + jnp.dot(h_ref[...], wdn_ref[...], preferred_element_type=F32)


def _ffn(x, g, w_up, conv_w, w_down):
    B, S, D = x.shape
    tm = ROW_TILE
    return pl.pallas_call(
        _ffn_kernel,
        grid=(B, S // tm),
        in_specs=[_rows(tm, D), _resident((1, D)), _resident(w_up.shape), _resident(conv_w.shape),
                  _resident(w_down.shape)],
        out_specs=_rows(tm, D),
        out_shape=jax.ShapeDtypeStruct((B, S, D), F32),
        scratch_shapes=[pltpu.VMEM((tm, D_FF), BF16), pltpu.VMEM((SUBLANES, 2 * D_FF), F32)],
        compiler_params=_params("parallel", "arbitrary"),
        name="conv_glu_ffn",
    )(x, g, w_up, conv_w, w_down)


def _rope_table_kernel(cos_ref, sin_ref):
    half = cos_ref.shape[1]
    pos = lax.broadcasted_iota(jnp.int32, cos_ref.shape, 0).astype(F32)
    j = lax.broadcasted_iota(jnp.int32, cos_ref.shape, 1).astype(F32)
    inv = jnp.exp(j * (-math.log(ROPE_BASE) / half))
    ang = pos * inv
    cos_ref[...] = jnp.cos(ang)
    sin_ref[...] = jnp.sin(ang)


def _rope_table(S):
    half = RET_KDIM // 2
    shape = jax.ShapeDtypeStruct((S, half), F32)
    return pl.pallas_call(_rope_table_kernel, out_shape=[shape, shape], name="rope_table")()


def _retproj_kernel(x_ref, g_ref, wq_ref, wk_ref, wv_ref, wg_ref, cos_ref, sin_ref,
                    q_ref, k_ref, v_ref, gate_ref):
    xn = _rms(x_ref[...], g_ref[...]).astype(BF16)
    cos = cos_ref[...]
    sin = sin_ref[...]
    half = RET_KDIM // 2

    def rotary_store(t, dst_ref):
        for h in range(RET_HEADS):
            x1 = t[:, h * RET_KDIM:h * RET_KDIM + half]
            x2 = t[:, h * RET_KDIM + half:(h + 1) * RET_KDIM]
            dst_ref[:, h * RET_KDIM:h * RET_KDIM + half] = (x1 * cos - x2 * sin).astype(BF16)
            dst_ref[:, h * RET_KDIM + half:(h + 1) * RET_KDIM] = (x1 * sin + x2 * cos).astype(BF16)

    rotary_store(jnp.dot(xn, wq_ref[...], preferred_element_type=F32), q_ref)
    rotary_store(jnp.dot(xn, wk_ref[...], preferred_element_type=F32) * (RET_KDIM ** -0.5), k_ref)
    v_ref[...] = jnp.dot(xn, wv_ref[...], preferred_element_type=F32).astype(BF16)
    gate_ref[...] = _silu(jnp.dot(xn, wg_ref[...], preferred_element_type=F32)).astype(BF16)


def _retproj(x, g, wq, wk, wv, wg, cos, sin):
    B, S, D = x.shape
    tm = ROW_TILE
    kw, vw = RET_HEADS * RET_KDIM, RET_HEADS * RET_VDIM
    table = pl.BlockSpec((tm, RET_KDIM // 2), lambda b, i: (i, 0))
    return pl.pallas_call(
        _retproj_kernel,
        grid=(B, S // tm),
        in_specs=[_rows(tm, D), _resident((1, D)), _resident(wq.shape), _resident(wk.shape),
                  _resident(wv.shape), _resident(wg.shape), table, table],
        out_specs=[_rows(tm, kw), _rows(tm, kw), _rows(tm, vw), _rows(tm, vw)],
        out_shape=[jax.ShapeDtypeStruct((B, S, kw), BF16), jax.ShapeDtypeStruct((B, S, kw), BF16),
                   jax.ShapeDtypeStruct((B, S, vw), BF16), jax.ShapeDtypeStruct((B, S, vw), BF16)],
        compiler_params=_params("parallel", "parallel"),
        name="ret_proj",
    )(x, g, wq, wk, wv, wg, cos, sin)


def _retention_kernel(q_ref, k_ref, v_ref, gate_ref, gn_ref, o_ref, state_ref):
    @pl.when(pl.program_id(1) == 0)
    def _():
        state_ref[...] = jnp.zeros_like(state_ref)

    C = RET_CHUNK
    ii = lax.broadcasted_iota(jnp.int32, (C, C), 0)
    jj = lax.broadcasted_iota(jnp.int32, (C, C), 1)
    diff = (ii - jj).astype(F32)
    pos = lax.broadcasted_iota(jnp.int32, (C, 1), 0).astype(F32)

    for h in range(RET_HEADS):
        log_g = math.log1p(-(2.0 ** (-5.0 - h)))
        inner_decay = jnp.where(diff >= 0, jnp.exp(log_g * jnp.maximum(diff, 0.0)), 0.0)
        q_decay = jnp.exp(log_g * (pos + 1.0))
        k_decay = jnp.exp(log_g * (C - 1.0 - pos))
        chunk_decay = math.exp(log_g * C)
        kcols = slice(h * RET_KDIM, (h + 1) * RET_KDIM)
        vcols = slice(h * RET_VDIM, (h + 1) * RET_VDIM)
        for c in range(q_ref.shape[0] // C):
            rows = slice(c * C, (c + 1) * C)
            qc = q_ref[rows, kcols]
            kc = k_ref[rows, kcols]
            vc = v_ref[rows, vcols]
            state = state_ref[h]
            scores = lax.dot_general(qc, kc, (((1,), (1,)), ((), ())),
                                     preferred_element_type=F32) * inner_decay
            o = (jnp.dot(scores.astype(BF16), vc, preferred_element_type=F32)
                 + jnp.dot(qc, state.astype(BF16), preferred_element_type=F32) * q_decay)
            kd = (kc.astype(F32) * k_decay).astype(BF16)
            state_ref[h] = state * chunk_decay + lax.dot_general(
                kd, vc, (((0,), (0,)), ((), ())), preferred_element_type=F32)
            mu = jnp.mean(o, axis=-1, keepdims=True)
            var = jnp.mean(jnp.square(o - mu), axis=-1, keepdims=True)
            on = (o - mu) * lax.rsqrt(var + EPS) * gn_ref[h:h + 1, :]
            o_ref[rows, vcols] = (gate_ref[rows, vcols].astype(F32) * on).astype(BF16)


def _retention(q, k, v, gate, gn_gain):
    B, S, kw = q.shape
    vw = v.shape[-1]
    tc = RET_TILE
    return pl.pallas_call(
        _retention_kernel,
        grid=(B, S // tc),
        in_specs=[_rows(tc, kw), _rows(tc, kw), _rows(tc, vw), _rows(tc, vw),
                  _resident(gn_gain.shape)],
        out_specs=_rows(tc, vw),
        out_shape=jax.ShapeDtypeStruct((B, S, vw), BF16),
        scratch_shapes=[pltpu.VMEM((RET_HEADS, RET_KDIM, RET_VDIM), F32)],
        compiler_params=_params("parallel", "arbitrary"),
        name="retention",
    )(q, k, v, gate, gn_gain)


def _proj_residual_kernel(x_ref, a_ref, w_ref, o_ref):
    o_ref[...] = x_ref[...] + jnp.dot(a_ref[...], w_ref[...], preferred_element_type=F32)


def _proj_residual(x, a, w):
    B, S, D = x.shape
    tm = ROW_TILE
    return pl.pallas_call(
        _proj_residual_kernel,
        grid=(B, S // tm),
        in_specs=[_rows(tm, D), _rows(tm, a.shape[-1]), _resident(w.shape)],
        out_specs=_rows(tm, D),
        out_shape=jax.ShapeDtypeStruct((B, S, D), F32),
        compiler_params=_params("parallel", "parallel"),
        name="ret_outproj",
    )(x, a, w)


def _head_sum_matrix():
    g = jnp.arange(A_WIDTH // 2) // A_HEAD_DIM
    return (g[:, None] == g[None, :]).astype(BF16)


def _head_expand_matrix():
    head_of_lane = jnp.arange(A_WIDTH) // A_HEAD_DIM
    return (jnp.arange(LANES)[:, None] == head_of_lane[None, :]).astype(BF16)


def kernel(x, even_norm, even_w_in, even_q_gain, even_k_gain, even_sconv_w, even_w_out, odd_norm,
           ret_wq, ret_wk, ret_wv, ret_wg, ret_gn_gain, ret_wo, ffn_norm, ffn_w_up, ffn_conv_w,
           ffn_w_down):
    B, S, D = x.shape
    depth = ffn_norm.shape[0]
    bf = lambda w: w.astype(BF16)
    row = lambda v: v.reshape(1, -1).astype(F32)
    hsum = _head_sum_matrix()
    expand = _head_expand_matrix()
    cos = sin = None

    for l in range(depth):
        if l % 2 == 0:
            e = l // 2
            q, k, v, b = _inproj(x, row(even_norm[e]), bf(even_w_in[e]),
                                 row(jnp.tile(even_q_gain[e], A_HEADS)),
                                 row(jnp.tile(even_k_gain[e], A_HEADS)),
                                 even_sconv_w[e], hsum)
            os, lses = zip(*[_dilated_attention_one(q, k, v, d) for d in DILATIONS])
            x = _outproj(x, os, lses, b, expand, bf(even_w_out[e]))
        else:
            o = l // 2
            if cos is None:
                cos, sin = _rope_table(S)
            q, k, v, gate = _retproj(x, row(odd_norm[o]), bf(ret_wq[o]), bf(ret_wk[o]),
                                     bf(ret_wv[o]), bf(ret_wg[o]), cos, sin)
            a = _retention(q, k, v, gate, ret_gn_gain[o])
            x = _proj_residual(x, a, bf(ret_wo[o]))
        x = _ffn(x, row(ffn_norm[l]), bf(ffn_w_up[l]), ffn_conv_w[l], bf(ffn_w_down[l]))
    return x
```

```python
import functools
import math

import jax
import jax.numpy as jnp
from jax import lax
from jax.experimental import pallas as pl
from jax.experimental.pallas import tpu as pltpu

F32 = jnp.float32
BF16 = jnp.bfloat16

D_MODEL = 1024
A_HEADS = 8
A_HEAD_DIM = 64
A_WIDTH = A_HEADS * A_HEAD_DIM
B_WIDTH = D_MODEL - A_WIDTH
DILATIONS = (1, 4, 16)
REACH = 128
RET_HEADS = 4
RET_KDIM = D_MODEL // RET_HEADS
RET_VDIM = 2 * D_MODEL // RET_HEADS
RET_CHUNK = 128
ROPE_BASE = 10000.0
D_FF = 2816
EPS = 1e-6

SUBLANES = 8
LANES = 128
VMEM_LIMIT_BYTES = 56 * 1024 * 1024

ROW_TILE = 512
FF_TILE = 256
ATTN_ROWS = 1024
RET_TILE = 256


def _params(*semantics):
    return pltpu.CompilerParams(dimension_semantics=semantics, vmem_limit_bytes=VMEM_LIMIT_BYTES)


def _resident(shape):
    nd = len(shape)
    return pl.BlockSpec(shape, lambda *_: (0,) * nd, pipeline_mode=pl.Buffered(1))


def _rows(tm, width):
    return pl.BlockSpec((None, tm, width), lambda b, i: (b, i, 0))


def _stream_rows(dil, tm, width):
    return pl.BlockSpec((None, dil, tm // dil, width), lambda b, i: (b, 0, i, 0))


def _rms(x, g):
    ms = jnp.mean(x * x, axis=-1, keepdims=True)
    return x * lax.rsqrt(ms + EPS) * g


def _silu(g):
    return g / (1.0 + jnp.exp(-g))


def _shift_rows(cur, prev, k):
    rolled = pltpu.roll(cur, k, axis=0)
    prev_rolled = pltpu.roll(prev, k, axis=0)
    row = lax.broadcasted_iota(jnp.int32, prev.shape, 0)
    head = jnp.where(row < k, prev_rolled, rolled[:SUBLANES])
    return jnp.concatenate([head, rolled[SUBLANES:]], axis=0)


def _causal_conv3(cur, prev, w):
    return (_shift_rows(cur, prev, 2) * w[0:1] + _shift_rows(cur, prev, 1) * w[1:2]
            + cur * w[2:3])


def _store_streams(t, slab_ref, out_refs):
    tm = t.shape[0]
    n_slabs = A_WIDTH // LANES
    if any(d > 1 for d in DILATIONS):
        for s in range(n_slabs):
            slab_ref[s] = t[:, s * LANES:(s + 1) * LANES]
    for dil, out_ref in zip(DILATIONS, out_refs):
        if dil == 1:
            out_ref[0] = t.astype(BF16)
            continue
        for r in range(dil):
            for s in range(n_slabs):
                out_ref[r, :, s * LANES:(s + 1) * LANES] = (
                    slab_ref[s, pl.ds(r, tm // dil, stride=dil), :].astype(BF16))


def _inproj_kernel(x_ref, g_ref, w_ref, qg_ref, kg_ref, cw_ref, hsum_ref, *rest):
    nd = len(DILATIONS)
    q_refs, k_refs, v_refs = rest[:nd], rest[nd:2 * nd], rest[2 * nd:3 * nd]
    b_ref, halo_ref, slab_ref = rest[3 * nd:]

    @pl.when(pl.program_id(1) == 0)
    def _():
        halo_ref[...] = jnp.zeros_like(halo_ref)

    xn = _rms(x_ref[...], g_ref[...]).astype(BF16)

    def proj(idx):
        return jnp.dot(xn, w_ref[:, idx * A_WIDTH:(idx + 1) * A_WIDTH], preferred_element_type=F32)

    def head_norm(t, gain):
        t2 = (t * t).astype(BF16)
        half = A_WIDTH // 2
        ss = jnp.concatenate(
            [jnp.dot(t2[:, :half], hsum_ref[...], preferred_element_type=F32),
             jnp.dot(t2[:, half:], hsum_ref[...], preferred_element_type=F32)], axis=1)
        return t * lax.rsqrt(ss * (1.0 / A_HEAD_DIM) + EPS) * gain

    _store_streams(head_norm(proj(0), qg_ref[...]) * (A_HEAD_DIM ** -0.5), slab_ref.at[0], q_refs)
    _store_streams(head_norm(proj(1), kg_ref[...]), slab_ref.at[1], k_refs)
    _store_streams(proj(2), slab_ref.at[2], v_refs)
    gate_b = proj(3)
    s = proj(4) * proj(5)
    conv = _causal_conv3(s, halo_ref[...], cw_ref[...])
    halo_ref[...] = s[s.shape[0] - SUBLANES:]
    b_ref[...] = (gate_b * conv).astype(BF16)


def _inproj(x, g, w_in, q_gain, k_gain, sconv_w, hsum):
    B, S, D = x.shape
    tm = ROW_TILE
    streams = [jax.ShapeDtypeStruct((B, d, S // d, A_WIDTH), BF16) for d in DILATIONS]
    stream_specs = [_stream_rows(d, tm, A_WIDTH) for d in DILATIONS]
    outs = pl.pallas_call(
        _inproj_kernel,
        grid=(B, S // tm),
        in_specs=[_rows(tm, D), _resident((1, D)), _resident(w_in.shape), _resident((1, A_WIDTH)),
                  _resident((1, A_WIDTH)), _resident(sconv_w.shape), _resident(hsum.shape)],
        out_specs=stream_specs * 3 + [_rows(tm, B_WIDTH)],
        out_shape=streams * 3 + [jax.ShapeDtypeStruct((B, S, B_WIDTH), BF16)],
        scratch_shapes=[pltpu.VMEM((SUBLANES, B_WIDTH), F32),
                        pltpu.VMEM((3, A_WIDTH // LANES, tm, LANES), F32)],
        compiler_params=_params("parallel", "arbitrary"),
        name="even_inproj",
    )(x, g, w_in, q_gain, k_gain, sconv_w, hsum)
    nd = len(DILATIONS)
    return outs[:nd], outs[nd:2 * nd], outs[2 * nd:3 * nd], outs[3 * nd]


def _col_reduce(x, op):
    parts = x.reshape(4, x.shape[0] // 4, x.shape[1])
    part = op(op(parts[0], parts[1]), op(parts[2], parts[3]))
    reduce = jnp.max if op is jnp.maximum else jnp.sum
    return reduce(part, axis=0, keepdims=True)


def _attn_kernel(q_ref, kp_ref, kc_ref, vp_ref, vc_ref, o_ref, lse_ref,
                 qt_scr, k_scr, vwin_scr, bias_scr, s_scr, ot_scr, lse_scr):
    first_tile = pl.program_id(2) == 0
    hd = A_HEAD_DIM
    n_pairs = A_WIDTH // LANES
    n_streams, tq = q_ref.shape[0], q_ref.shape[1]
    blocks_per_stream = tq // REACH
    n_blocks = n_streams * blocks_per_stream
    assert blocks_per_stream & (blocks_per_stream - 1) == 0 and n_blocks % 2 == 0

    kj = lax.broadcasted_iota(jnp.int32, (2 * REACH, 2 * REACH), 0)
    qi = lax.broadcasted_iota(jnp.int32, (2 * REACH, 2 * REACH), 1) % REACH
    band = (kj >= qi) & (kj <= qi + REACH)
    bias_scr[0] = jnp.where(band & ((kj >= REACH) | jnp.logical_not(first_tile)), 0.0, -1e30)
    bias_scr[1] = jnp.where(band, 0.0, -1e30)
    for s in range(n_streams):
        q_t = q_ref[s].T
        v_t = jnp.concatenate([vp_ref[s], vc_ref[s]], axis=0).T
        for j in range(blocks_per_stream):
            qt_scr[s * blocks_per_stream + j] = q_t[:, j * REACH:(j + 1) * REACH]
            vwin_scr[s * blocks_per_stream + j] = v_t[:, j * REACH:(j + 2) * REACH]
        for pr in range(n_pairs):
            slab = slice(pr * LANES, (pr + 1) * LANES)
            k_scr[pr, s, 0:REACH, :] = kp_ref[s, :, slab]
            k_scr[pr, s, REACH:, :] = kc_ref[s, :, slab]
    zeros = jnp.zeros((hd, REACH), BF16)

    def block_scores(blk, slot):
        s = lax.shift_right_logical(blk, blocks_per_stream.bit_length() - 1)
        j = blk & (blocks_per_stream - 1)
        bias = bias_scr[jnp.minimum(j, 1)]
        maxima = []
        for pr in range(n_pairs):
            k_win = k_scr[pr, s, pl.ds(pl.multiple_of(j * REACH, REACH), 2 * REACH), :]
            qt = qt_scr[blk, pr * LANES:(pr + 1) * LANES, :]
            q_bd = jnp.concatenate([jnp.concatenate([qt[:hd], zeros], axis=1),
                                    jnp.concatenate([zeros, qt[hd:]], axis=1)], axis=0)
            sc = jnp.dot(k_win, q_bd, preferred_element_type=F32) + bias
            s_scr[slot, pr] = sc
            maxima.append(_col_reduce(sc, jnp.maximum))
        return tuple(maxima)

    def block_finish(blk, slot, maxima):
        for pr in range(n_pairs):
            p_t = jnp.exp(s_scr[slot, pr] - maxima[pr])
            den = _col_reduce(p_t, jnp.add)
            v_pair = vwin_scr[blk, pr * LANES:(pr + 1) * LANES, :]
            o_t = jnp.dot(v_pair, p_t.astype(BF16), preferred_element_type=F32) / den
            ot_scr[blk, pr * LANES:(pr + 1) * LANES, :] = jnp.concatenate(
                [o_t[:hd, :REACH], o_t[hd:, REACH:]], axis=0)
            lse_scr[blk, pr] = jnp.broadcast_to(maxima[pr] + jnp.log(den), (SUBLANES, 2 * REACH))

    def two_blocks(u, maxima_even):
        blk = 2 * u
        maxima_odd = block_scores(blk + 1, 1)
        block_finish(blk, 0, maxima_even)
        maxima_next = block_scores(jnp.minimum(blk + 2, n_blocks - 1), 0)
        block_finish(blk + 1, 1, maxima_odd)
        return maxima_next

    lax.fori_loop(0, n_blocks // 2, two_blocks, block_scores(jnp.int32(0), 0))

    for blk in range(n_blocks):
        s, j = divmod(blk, blocks_per_stream)
        rows = slice(j * REACH, (j + 1) * REACH)
        o_ref[s, rows, :] = ot_scr[blk].T.astype(BF16)
        lse_rows = []
        for pr in range(n_pairs):
            lse_rows += [lse_scr[blk, pr, 0:1, :REACH], lse_scr[blk, pr, 0:1, REACH:]]
        lse_rows.append(jnp.zeros((LANES - A_HEADS, REACH), F32))
        lse_ref[s, rows, :] = jnp.concatenate(lse_rows, axis=0).T


def _dilated_attention_one(q, k, v, dil):
    B, _, L, W = q.shape
    tq = min(ATTN_ROWS, L)
    ns = min(ATTN_ROWS // tq, dil)
    n_blocks, n_pairs = ns * tq // REACH, W // LANES
    cur = pl.BlockSpec((None, ns, tq, W), lambda b, r, i: (b, r, i, 0))
    prev = pl.BlockSpec((None, ns, REACH, W),
                        lambda b, r, i: (b, r, jnp.maximum(i * (tq // REACH) - 1, 0), 0))
    return pl.pallas_call(
        _attn_kernel,
        grid=(B, dil // ns, L // tq),
        in_specs=[cur, prev, cur, prev, cur],
        out_specs=[cur, pl.BlockSpec((None, ns, tq, LANES), lambda b, r, i: (b, r, i, 0))],
        out_shape=[jax.ShapeDtypeStruct((B, dil, L, W), BF16),
                   jax.ShapeDtypeStruct((B, dil, L, LANES), F32)],
        scratch_shapes=[
            pltpu.VMEM((n_blocks, W, REACH), BF16),
            pltpu.VMEM((n_pairs, ns, REACH + tq, LANES), BF16),
            pltpu.VMEM((n_blocks, W, 2 * REACH), BF16),
            pltpu.VMEM((2, 2 * REACH, 2 * REACH), F32),
            pltpu.VMEM((2, n_pairs, 2 * REACH, 2 * REACH), F32),
            pltpu.VMEM((n_blocks, W, REACH), F32),
            pltpu.VMEM((n_blocks, n_pairs, SUBLANES, 2 * REACH), F32),
        ],
        compiler_params=_params("parallel", "parallel", "arbitrary"),
        name=f"dilated_attn_d{dil}",
    )(q, k, k, v, v)


def _load_streams(src_ref, dil, slab_ref):
    n_slabs = src_ref.shape[-1] // LANES
    if dil == 1:
        t = src_ref[0].astype(F32)
        return [t[:, s * LANES:(s + 1) * LANES] for s in range(n_slabs)]
    n = src_ref.shape[1]
    for r in range(dil):
        for s in range(n_slabs):
            slab_ref[s, pl.ds(r, n, stride=dil), :] = (
                src_ref[r, :, s * LANES:(s + 1) * LANES].astype(F32))
    return [slab_ref[s] for s in range(n_slabs)]


def _outproj_kernel(x_ref, *rest):
    nd = len(DILATIONS)
    o_refs, l_refs = rest[:nd], rest[nd:2 * nd]
    b_ref, expand_ref, w_ref, out_ref, o_slab_ref, l_slab_ref = rest[2 * nd:]
    n_slabs = A_WIDTH // LANES

    ls = [_load_streams(l_ref, d, l_slab_ref.at[n])[0]
          for n, (d, l_ref) in enumerate(zip(DILATIONS, l_refs))]
    mx = functools.reduce(jnp.maximum, ls)
    es = [jnp.exp(l - mx) for l in ls]
    tot = functools.reduce(jnp.add, es)
    a_slabs = [None] * n_slabs
    for n, (d, e, o_ref) in enumerate(zip(DILATIONS, es, o_refs)):
        w = e / tot
        hi = w.astype(BF16)
        lo = (w - hi.astype(F32)).astype(BF16)
        wf = (jnp.dot(hi, expand_ref[...], preferred_element_type=F32)
              + jnp.dot(lo, expand_ref[...], preferred_element_type=F32))
        o_slabs = _load_streams(o_ref, d, o_slab_ref.at[n])
        for s in range(n_slabs):
            term = wf[:, s * LANES:(s + 1) * LANES] * o_slabs[s]
            a_slabs[s] = term if a_slabs[s] is None else a_slabs[s] + term
    a = jnp.concatenate(a_slabs, axis=1).astype(BF16)
    y = (jnp.dot(a, w_ref[:A_WIDTH, :], preferred_element_type=F32)
         + jnp.dot(b_ref[...], w_ref[A_WIDTH:, :], preferred_element_type=F32))
    out_ref[...] = x_ref[...] + y


def _outproj(x, os, lses, b, expand, w_out):
    B, S, D = x.shape
    tm = ROW_TILE
    nd = len(DILATIONS)
    return pl.pallas_call(
        _outproj_kernel,
        grid=(B, S // tm),
        in_specs=[_rows(tm, D)] + [_stream_rows(d, tm, A_WIDTH) for d in DILATIONS]
                 + [_stream_rows(d, tm, LANES) for d in DILATIONS]
                 + [_rows(tm, B_WIDTH), _resident(expand.shape), _resident(w_out.shape)],
        out_specs=_rows(tm, D),
        out_shape=jax.ShapeDtypeStruct((B, S, D), F32),
        scratch_shapes=[pltpu.VMEM((nd, A_WIDTH // LANES, tm, LANES), F32),
                        pltpu.VMEM((nd, 1, tm, LANES), F32)],
        compiler_params=_params("parallel", "parallel"),
        name="even_outproj",
    )(x, *os, *lses, b, expand, w_out)


def _ffn_kernel(x_ref, g_ref, wup_ref, cw_ref, wdn_ref, o_ref, h_ref, halo_ref):
    @pl.when(pl.program_id(1) == 0)
    def _():
        halo_ref[...] = jnp.zeros_like(halo_ref)

    x = x_ref[...]
    tm = x.shape[0]
    xn = _rms(x, g_ref[...]).astype(BF16)

    def conv_up(col):
        cols = slice(col, col + FF_TILE)
        u = jnp.dot(xn, wup_ref[:, cols], preferred_element_type=F32)
        y = _causal_conv3(u, halo_ref[:, cols], cw_ref[:, cols])
        halo_ref[:, cols] = u[tm - SUBLANES:]
        return y

    for c in range(D_FF // FF_TILE):
        gate = conv_up(c * FF_TILE)
        val = conv_up(D_FF + c * FF_TILE)
        h_ref[:, c * FF_TILE:(c + 1) * FF_TILE] = (_silu(gate) * val).astype(BF16)
    o_ref[...] = x + jnp.dot(h_ref[...], wdn_ref[...], preferred_element_type=F32)


def _ffn(x, g, w_up, conv_w, w_down):
    B, S, D = x.shape
    tm = ROW_TILE
    return pl.pallas_call(
        _ffn_kernel,
        grid=(B, S // tm),
        in_specs=[_rows(tm, D), _resident((1, D)), _resident(w_up.shape), _resident(conv_w.shape),
                  _resident(w_down.shape)],
        out_specs=_rows(tm, D),
        out_shape=jax.ShapeDtypeStruct((B, S, D), F32),
        scratch_shapes=[pltpu.VMEM((tm, D_FF), BF16), pltpu.VMEM((SUBLANES, 2 * D_FF), F32)],
        compiler_params=_params("parallel", "arbitrary"),
        name="conv_glu_ffn",
    )(x, g, w_up, conv_w, w_down)


def _rope_table_kernel(cos_ref, sin_ref):
    half = cos_ref.shape[1]
    pos = lax.broadcasted_iota(jnp.int32, cos_ref.shape, 0).astype(F32)
    j = lax.broadcasted_iota(jnp.int32, cos_ref.shape, 1).astype(F32)
    inv = jnp.exp(j * (-math.log(ROPE_BASE) / half))
    ang = pos * inv
    cos_ref[...] = jnp.cos(ang)
    sin_ref[...] = jnp.sin(ang)


def _rope_table(S):
    half = RET_KDIM // 2
    shape = jax.ShapeDtypeStruct((S, half), F32)
    return pl.pallas_call(_rope_table_kernel, out_shape=[shape, shape], name="rope_table")()


def _retproj_kernel(x_ref, g_ref, wq_ref, wk_ref, wv_ref, wg_ref, cos_ref, sin_ref,
                    q_ref, k_ref, v_ref, gate_ref):
    xn = _rms(x_ref[...], g_ref[...]).astype(BF16)
    cos = cos_ref[...]
    sin = sin_ref[...]
    half = RET_KDIM // 2

    def rotary_store(t, dst_ref):
        for h in range(RET_HEADS):
            x1 = t[:, h * RET_KDIM:h * RET_KDIM + half]
            x2 = t[:, h * RET_KDIM + half:(h + 1) * RET_KDIM]
            dst_ref[:, h * RET_KDIM:h * RET_KDIM + half] = (x1 * cos - x2 * sin).astype(BF16)
            dst_ref[:, h * RET_KDIM + half:(h + 1) * RET_KDIM] = (x1 * sin + x2 * cos).astype(BF16)

    rotary_store(jnp.dot(xn, wq_ref[...], preferred_element_type=F32), q_ref)
    rotary_store(jnp.dot(xn, wk_ref[...], preferred_element_type=F32) * (RET_KDIM ** -0.5), k_ref)
    v_ref[...] = jnp.dot(xn, wv_ref[...], preferred_element_type=F32).astype(BF16)
    gate_ref[...] = _silu(jnp.dot(xn, wg_ref[...], preferred_element_type=F32)).astype(BF16)


def _retproj(x, g, wq, wk, wv, wg, cos, sin):
    B, S, D = x.shape
    tm = ROW_TILE
    kw, vw = RET_HEADS * RET_KDIM, RET_HEADS * RET_VDIM
    table = pl.BlockSpec((tm, RET_KDIM // 2), lambda b, i: (i, 0))
    return pl.pallas_call(
        _retproj_kernel,
        grid=(B, S // tm),
        in_specs=[_rows(tm, D), _resident((1, D)), _resident(wq.shape), _resident(wk.shape),
                  _resident(wv.shape), _resident(wg.shape), table, table],
        out_specs=[_rows(tm, kw), _rows(tm, kw), _rows(tm, vw), _rows(tm, vw)],
        out_shape=[jax.ShapeDtypeStruct((B, S, kw), BF16), jax.ShapeDtypeStruct((B, S, kw), BF16),
                   jax.ShapeDtypeStruct((B, S, vw), BF16), jax.ShapeDtypeStruct((B, S, vw), BF16)],
        compiler_params=_params("parallel", "parallel"),
        name="ret_proj",
    )(x, g, wq, wk, wv, wg, cos, sin)


def _retention_kernel(q_ref, k_ref, v_ref, gate_ref, gn_ref, o_ref, state_ref):
    @pl.when(pl.program_id(1) == 0)
    def _():
        state_ref[...] = jnp.zeros_like(state_ref)

    C = RET_CHUNK
    ii = lax.broadcasted_iota(jnp.int32, (C, C), 0)
    jj = lax.broadcasted_iota(jnp.int32, (C, C), 1)
    diff = (ii - jj).astype(F32)
    pos = lax.broadcasted_iota(jnp.int32, (C, 1), 0).astype(F32)

    for h in range(RET_HEADS):
        log_g = math.log1p(-(2.0 ** (-5.0 - h)))
        inner_decay = jnp.where(diff >= 0, jnp.exp(log_g * jnp.maximum(diff, 0.0)), 0.0)
        q_decay = jnp.exp(log_g * (pos + 1.0))
        k_decay = jnp.exp(log_g * (C - 1.0 - pos))
        chunk_decay = math.exp(log_g * C)
        kcols = slice(h * RET_KDIM, (h + 1) * RET_KDIM)
        vcols = slice(h * RET_VDIM, (h + 1) * RET_VDIM)
        for c in range(q_ref.shape[0] // C):
            rows = slice(c * C, (c + 1) * C)
            qc = q_ref[rows, kcols]
            kc = k_ref[rows, kcols]
            vc = v_ref[rows, vcols]
            state = state_ref[h]
            scores = lax.dot_general(qc, kc, (((1,), (1,)), ((), ())),
                                     preferred_element_type=F32) * inner_decay
            o = (jnp.dot(scores.astype(BF16), vc, preferred_element_type=F32)
                 + jnp.dot(qc, state.astype(BF16), preferred_element_type=F32) * q_decay)
            kd = (kc.astype(F32) * k_decay).astype(BF16)
            state_ref[h] = state * chunk_decay + lax.dot_general(
                kd, vc, (((0,), (0,)), ((), ())), preferred_element_type=F32)
            mu = jnp.mean(o, axis=-1, keepdims=True)
            var = jnp.mean(jnp.square(o - mu), axis=-1, keepdims=True)
            on = (o - mu) * lax.rsqrt(var + EPS) * gn_ref[h:h + 1, :]
            o_ref[rows, vcols] = (gate_ref[rows, vcols].astype(F32) * on).astype(BF16)


def _retention(q, k, v, gate, gn_gain):
    B, S, kw = q.shape
    vw = v.shape[-1]
    tc = RET_TILE
    return pl.pallas_call(
        _retention_kernel,
        grid=(B, S // tc),
        in_specs=[_rows(tc, kw), _rows(tc, kw), _rows(tc, vw), _rows(tc, vw),
                  _resident(gn_gain.shape)],
        out_specs=_rows(tc, vw),
        out_shape=jax.ShapeDtypeStruct((B, S, vw), BF16),
        scratch_shapes=[pltpu.VMEM((RET_HEADS, RET_KDIM, RET_VDIM), F32)],
        compiler_params=_params("parallel", "arbitrary"),
        name="retention",
    )(q, k, v, gate, gn_gain)


def _proj_residual_kernel(x_ref, a_ref, w_ref, o_ref):
    o_ref[...] = x_ref[...] + jnp.dot(a_ref[...], w_ref[...], preferred_element_type=F32)


def _proj_residual(x, a, w):
    B, S, D = x.shape
    tm = ROW_TILE
    return pl.pallas_call(
        _proj_residual_kernel,
        grid=(B, S // tm),
        in_specs=[_rows(tm, D), _rows(tm, a.shape[-1]), _resident(w.shape)],
        out_specs=_rows(tm, D),
        out_shape=jax.ShapeDtypeStruct((B, S, D), F32),
        compiler_params=_params("parallel", "parallel"),
        name="ret_outproj",
    )(x, a, w)


def _head_sum_matrix():
    g = jnp.arange(A_WIDTH // 2) // A_HEAD_DIM
    return (g[:, None] == g[None, :]).astype(BF16)


def _head_expand_matrix():
    head_of_lane = jnp.arange(A_WIDTH) // A_HEAD_DIM
    return (jnp.arange(LANES)[:, None] == head_of_lane[None, :]).astype(BF16)


def kernel(x, even_norm, even_w_in, even_q_gain, even_k_gain, even_sconv_w, even_w_out, odd_norm,
           ret_wq, ret_wk, ret_wv, ret_wg, ret_gn_gain, ret_wo, ffn_norm, ffn_w_up, ffn_conv_w,
           ffn_w_down):
    B, S, D = x.shape
    depth = ffn_norm.shape[0]
    bf = lambda w: w.astype(BF16)
    row = lambda v: v.reshape(1, -1).astype(F32)
    hsum = _head_sum_matrix()
    expand = _head_expand_matrix()
    cos = sin = None

    for l in range(depth):
        if l % 2 == 0:
            e = l // 2
            qs, ks, vs, b = _inproj(x, row(even_norm[e]), bf(even_w_in[e]),
                                    row(jnp.tile(even_q_gain[e], A_HEADS)),
                                    row(jnp.tile(even_k_gain[e], A_HEADS)),
                                    even_sconv_w[e], hsum)
            os, lses = zip(*[_dilated_attention_one(q, k, v, d)
                             for q, k, v, d in zip(qs, ks, vs, DILATIONS)])
            x = _outproj(x, os, lses, b, expand, bf(even_w_out[e]))
        else:
            o = l // 2
            if cos is None:
                cos, sin = _rope_table(S)
            q, k, v, gate = _retproj(x, row(odd_norm[o]), bf(ret_wq[o]), bf(ret_wk[o]),
                                     bf(ret_wv[o]), bf(ret_wg[o]), cos, sin)
            a = _retention(q, k, v, gate, ret_gn_gain[o])
            x = _proj_residual(x, a, bf(ret_wo[o]))
        x = _ffn(x, row(ffn_norm[l]), bf(ffn_w_up[l]), ffn_conv_w[l], bf(ffn_w_down[l]))
    return x
```

```python
import functools
import math

import jax
import jax.numpy as jnp
from jax import lax
from jax.experimental import pallas as pl
from jax.experimental.pallas import tpu as pltpu

F32 = jnp.float32
BF16 = jnp.bfloat16

D_MODEL = 1024
A_HEADS = 8
A_HEAD_DIM = 64
A_WIDTH = A_HEADS * A_HEAD_DIM
B_WIDTH = D_MODEL - A_WIDTH
DILATIONS = (1, 4, 16)
REACH = 128
RET_HEADS = 4
RET_KDIM = D_MODEL // RET_HEADS
RET_VDIM = 2 * D_MODEL // RET_HEADS
RET_CHUNK = 256
ROPE_BASE = 10000.0
D_FF = 2816
EPS = 1e-6

SUBLANES = 8
LANES = 128
VMEM_LIMIT_BYTES = 56 * 1024 * 1024

ROW_TILE = 512
FFN_ROW_TILE = 1024
FF_TILE = 256
ATTN_ROWS = 1024
RET_TILE = 256


def _params(*semantics):
    return pltpu.CompilerParams(dimension_semantics=semantics, vmem_limit_bytes=VMEM_LIMIT_BYTES)


def _resident(shape):
    nd = len(shape)
    return pl.BlockSpec(shape, lambda *_: (0,) * nd, pipeline_mode=pl.Buffered(1))


def _rows(tm, width):
    return pl.BlockSpec((None, tm, width), lambda b, i: (b, i, 0))


def _stream_rows(dil, tm, width):
    return pl.BlockSpec((None, dil, tm // dil, width), lambda b, i: (b, 0, i, 0))


def _rms(x, g):
    ms = jnp.mean(x * x, axis=-1, keepdims=True)
    return x * lax.rsqrt(ms + EPS) * g


def _silu(g):
    return g / (1.0 + jnp.exp2(g * -math.log2(math.e)))


def _shift_rows(cur, prev, k):
    rolled = pltpu.roll(cur, k, axis=0)
    prev_rolled = pltpu.roll(prev, k, axis=0)
    row = lax.broadcasted_iota(jnp.int32, prev.shape, 0)
    head = jnp.where(row < k, prev_rolled, rolled[:SUBLANES])
    return jnp.concatenate([head, rolled[SUBLANES:]], axis=0)


def _causal_conv3(cur, prev, w):
    return (_shift_rows(cur, prev, 2) * w[0:1] + _shift_rows(cur, prev, 1) * w[1:2]
            + cur * w[2:3])


def _store_streams(t, slab_ref, out_refs):
    tm = t.shape[0]
    n_slabs = A_WIDTH // LANES
    if any(d > 1 for d in DILATIONS):
        for s in range(n_slabs):
            slab_ref[s] = t[:, s * LANES:(s + 1) * LANES]
    for dil, out_ref in zip(DILATIONS, out_refs):
        if dil == 1:
            out_ref[0] = t.astype(BF16)
            continue
        for r in range(dil):
            for s in range(n_slabs):
                out_ref[r, :, s * LANES:(s + 1) * LANES] = (
                    slab_ref[s, pl.ds(r, tm // dil, stride=dil), :].astype(BF16))


def _inproj_kernel(x_ref, g_ref, w_ref, qg_ref, kg_ref, cw_ref, hsum_ref, *rest):
    nd = len(DILATIONS)
    q_refs, k_refs, v_refs = rest[:nd], rest[nd:2 * nd], rest[2 * nd:3 * nd]
    b_ref, halo_ref, slab_ref = rest[3 * nd:]

    @pl.when(pl.program_id(1) == 0)
    def _():
        halo_ref[...] = jnp.zeros_like(halo_ref)

    xn = _rms(x_ref[...], g_ref[...]).astype(BF16)

    def proj(idx):
        return jnp.dot(xn, w_ref[:, idx * A_WIDTH:(idx + 1) * A_WIDTH], preferred_element_type=F32)

    def head_norm(t, gain):
        t2 = (t * t).astype(BF16)
        half = A_WIDTH // 2
        ss = jnp.concatenate(
            [jnp.dot(t2[:, :half], hsum_ref[...], preferred_element_type=F32),
             jnp.dot(t2[:, half:], hsum_ref[...], preferred_element_type=F32)], axis=1)
        return t * lax.rsqrt(ss * (1.0 / A_HEAD_DIM) + EPS) * gain

    _store_streams(head_norm(proj(0), qg_ref[...]), slab_ref.at[0], q_refs)
    _store_streams(head_norm(proj(1), kg_ref[...]), slab_ref.at[1], k_refs)
    _store_streams(proj(2), slab_ref.at[2], v_refs)
    gate_b = proj(3)
    s = proj(4) * proj(5)
    conv = _causal_conv3(s, halo_ref[...], cw_ref[...])
    halo_ref[...] = s[s.shape[0] - SUBLANES:]
    b_ref[...] = (gate_b * conv).astype(BF16)


def _inproj(x, g, w_in, q_gain, k_gain, sconv_w, hsum):
    B, S, D = x.shape
    tm = ROW_TILE
    streams = [jax.ShapeDtypeStruct((B, d, S // d, A_WIDTH), BF16) for d in DILATIONS]
    stream_specs = [_stream_rows(d, tm, A_WIDTH) for d in DILATIONS]
    outs = pl.pallas_call(
        _inproj_kernel,
        grid=(B, S // tm),
        in_specs=[_rows(tm, D), _resident((1, D)), _resident(w_in.shape), _resident((1, A_WIDTH)),
                  _resident((1, A_WIDTH)), _resident(sconv_w.shape), _resident(hsum.shape)],
        out_specs=stream_specs * 3 + [_rows(tm, B_WIDTH)],
        out_shape=streams * 3 + [jax.ShapeDtypeStruct((B, S, B_WIDTH), BF16)],
        scratch_shapes=[pltpu.VMEM((SUBLANES, B_WIDTH), F32),
                        pltpu.VMEM((3, A_WIDTH // LANES, tm, LANES), F32)],
        compiler_params=_params("parallel", "arbitrary"),
        name="even_inproj",
    )(x, g, w_in, q_gain, k_gain, sconv_w, hsum)
    nd = len(DILATIONS)
    return outs[:nd], outs[nd:2 * nd], outs[2 * nd:3 * nd], outs[3 * nd]


_Q_SCALE = A_HEAD_DIM ** -0.5 * math.log2(math.e)


def _col_reduce(x, op):
    parts = x.reshape(4, x.shape[0] // 4, x.shape[1])
    part = op(op(parts[0], parts[1]), op(parts[2], parts[3]))
    reduce = jnp.max if op is jnp.maximum else jnp.sum
    return reduce(part, axis=0, keepdims=True)


def _attn_kernel(q_ref, kp_ref, kc_ref, vp_ref, vc_ref, o_ref, lse_ref,
                 qt_scr, k_scr, vwin_scr, bias_scr, s_scr, ot_scr, lse_scr):
    first_tile = pl.program_id(2) == 0
    hd = A_HEAD_DIM
    n_pairs = A_WIDTH // LANES
    n_streams, tq = q_ref.shape[0], q_ref.shape[1]
    blocks_per_stream = tq // REACH
    n_blocks = n_streams * blocks_per_stream
    assert blocks_per_stream & (blocks_per_stream - 1) == 0 and n_blocks % 2 == 0

    kj = lax.broadcasted_iota(jnp.int32, (2 * REACH, 2 * REACH), 0)
    qi = lax.broadcasted_iota(jnp.int32, (2 * REACH, 2 * REACH), 1) % REACH
    band = (kj >= qi) & (kj <= qi + REACH)
    bias_scr[0] = jnp.where(band & ((kj >= REACH) | jnp.logical_not(first_tile)), 0.0, -1e30)
    bias_scr[1] = jnp.where(band, 0.0, -1e30)
    for s in range(n_streams):
        q_t = q_ref[s].T
        v_t = jnp.concatenate([vp_ref[s], vc_ref[s]], axis=0).T
        for j in range(blocks_per_stream):
            qt_scr[s * blocks_per_stream + j] = q_t[:, j * REACH:(j + 1) * REACH]
            vwin_scr[s * blocks_per_stream + j] = v_t[:, j * REACH:(j + 2) * REACH]
        for pr in range(n_pairs):
            slab = slice(pr * LANES, (pr + 1) * LANES)
            k_scr[pr, s, 0:REACH, :] = kp_ref[s, :, slab]
            k_scr[pr, s, REACH:, :] = kc_ref[s, :, slab]
    zeros = jnp.zeros((hd, REACH), BF16)

    def block_scores(blk, slot):
        s = lax.shift_right_logical(blk, blocks_per_stream.bit_length() - 1)
        j = blk & (blocks_per_stream - 1)
        bias = bias_scr[jnp.minimum(j, 1)]
        maxima = []
        for pr in range(n_pairs):
            k_win = k_scr[pr, s, pl.ds(pl.multiple_of(j * REACH, REACH), 2 * REACH), :]
            qt = qt_scr[blk, pr * LANES:(pr + 1) * LANES, :]
            q_bd = jnp.concatenate([jnp.concatenate([qt[:hd], zeros], axis=1),
                                    jnp.concatenate([zeros, qt[hd:]], axis=1)], axis=0)
            sc = jnp.dot(k_win, q_bd, preferred_element_type=F32) + bias
            s_scr[slot, pr] = sc
            maxima.append(_col_reduce(sc, jnp.maximum))
        return tuple(maxima)

    def block_finish(blk, slot, maxima):
        for pr in range(n_pairs):
            p_t = jnp.exp2(s_scr[slot, pr] - maxima[pr])
            den = _col_reduce(p_t, jnp.add)
            v_pair = vwin_scr[blk, pr * LANES:(pr + 1) * LANES, :]
            o_t = jnp.dot(v_pair, p_t.astype(BF16), preferred_element_type=F32) / den
            ot_scr[blk, pr * LANES:(pr + 1) * LANES, :] = jnp.concatenate(
                [o_t[:hd, :REACH], o_t[hd:, REACH:]], axis=0).astype(BF16)
            lse = (maxima[pr] + jnp.log2(den)) * math.log(2.0)
            lse_scr[blk, pr] = jnp.broadcast_to(lse, (SUBLANES, 2 * REACH))

    def two_blocks(u, maxima_even, lookahead=True):
        blk = 2 * u
        maxima_odd = block_scores(blk + 1, 1)
        block_finish(blk, 0, maxima_even)
        maxima_next = block_scores(blk + 2, 0) if lookahead else None
        block_finish(blk + 1, 1, maxima_odd)
        return maxima_next

    maxima = lax.fori_loop(0, n_blocks // 2 - 1, two_blocks, block_scores(jnp.int32(0), 0))
    two_blocks(jnp.int32(n_blocks // 2 - 1), maxima, lookahead=False)

    for blk in range(n_blocks):
        s, j = divmod(blk, blocks_per_stream)
        rows = slice(j * REACH, (j + 1) * REACH)
        o_ref[s, rows, :] = ot_scr[blk].T
        lse_rows = []
        for pr in range(n_pairs):
            lse_rows += [lse_scr[blk, pr, 0:1, :REACH], lse_scr[blk, pr, 0:1, REACH:]]
        lse_rows.append(jnp.zeros((LANES - A_HEADS, REACH), F32))
        lse_ref[s, rows, :] = jnp.concatenate(lse_rows, axis=0).T


def _dilated_attention_one(q, k, v, dil):
    B, _, L, W = q.shape
    tq = min(ATTN_ROWS, L)
    ns = min(ATTN_ROWS // tq, dil)
    n_blocks, n_pairs = ns * tq // REACH, W // LANES
    cur = pl.BlockSpec((None, ns, tq, W), lambda b, r, i: (b, r, i, 0))
    prev = pl.BlockSpec((None, ns, REACH, W),
                        lambda b, r, i: (b, r, jnp.maximum(i * (tq // REACH) - 1, 0), 0))
    return pl.pallas_call(
        _attn_kernel,
        grid=(B, dil // ns, L // tq),
        in_specs=[cur, prev, cur, prev, cur],
        out_specs=[cur, pl.BlockSpec((None, ns, tq, LANES), lambda b, r, i: (b, r, i, 0))],
        out_shape=[jax.ShapeDtypeStruct((B, dil, L, W), BF16),
                   jax.ShapeDtypeStruct((B, dil, L, LANES), F32)],
        scratch_shapes=[
            pltpu.VMEM((n_blocks, W, REACH), BF16),
            pltpu.VMEM((n_pairs, ns, REACH + tq, LANES), BF16),
            pltpu.VMEM((n_blocks, W, 2 * REACH), BF16),
            pltpu.VMEM((2, 2 * REACH, 2 * REACH), F32),
            pltpu.VMEM((2, n_pairs, 2 * REACH, 2 * REACH), F32),
            pltpu.VMEM((n_blocks, W, REACH), BF16),
            pltpu.VMEM((n_blocks, n_pairs, SUBLANES, 2 * REACH), F32),
        ],
        compiler_params=_params("parallel", "parallel", "arbitrary"),
        name=f"dilated_attn_d{dil}",
    )(q, k, k, v, v)


def _load_streams(src_ref, dil, slab_ref):
    n_slabs = src_ref.shape[-1] // LANES
    if dil == 1:
        t = src_ref[0].astype(F32)
        return [t[:, s * LANES:(s + 1) * LANES] for s in range(n_slabs)]
    n = src_ref.shape[1]
    for r in range(dil):
        for s in range(n_slabs):
            slab_ref[s, pl.ds(r, n, stride=dil), :] = (
                src_ref[r, :, s * LANES:(s + 1) * LANES].astype(F32))
    return [slab_ref[s] for s in range(n_slabs)]


def _outproj_kernel(x_ref, *rest):
    nd = len(DILATIONS)
    o_refs, l_refs = rest[:nd], rest[nd:2 * nd]
    b_ref, expand_ref, w_ref, out_ref, o_slab_ref, l_slab_ref = rest[2 * nd:]
    n_slabs = A_WIDTH // LANES

    ls = [_load_streams(l_ref, d, l_slab_ref.at[n])[0]
          for n, (d, l_ref) in enumerate(zip(DILATIONS, l_refs))]
    mx = functools.reduce(jnp.maximum, ls)
    es = [jnp.exp(l - mx) for l in ls]
    tot = functools.reduce(jnp.add, es)
    a_slabs = [None] * n_slabs
    for n, (d, e, o_ref) in enumerate(zip(DILATIONS, es, o_refs)):
        w = e / tot
        hi = w.astype(BF16)
        lo = (w - hi.astype(F32)).astype(BF16)
        wf = (jnp.dot(hi, expand_ref[...], preferred_element_type=F32)
              + jnp.dot(lo, expand_ref[...], preferred_element_type=F32))
        o_slabs = _load_streams(o_ref, d, o_slab_ref.at[n])
        for s in range(n_slabs):
            term = wf[:, s * LANES:(s + 1) * LANES] * o_slabs[s]
            a_slabs[s] = term if a_slabs[s] is None else a_slabs[s] + term
    a = jnp.concatenate(a_slabs, axis=1).astype(BF16)
    y = (jnp.dot(a, w_ref[:A_WIDTH, :], preferred_element_type=F32)
         + jnp.dot(b_ref[...], w_ref[A_WIDTH:, :], preferred_element_type=F32))
    out_ref[...] = x_ref[...] + y


def _outproj(x, os, lses, b, expand, w_out):
    B, S, D = x.shape
    tm = ROW_TILE
    nd = len(DILATIONS)
    return pl.pallas_call(
        _outproj_kernel,
        grid=(B, S // tm),
        in_specs=[_rows(tm, D)] + [_stream_rows(d, tm, A_WIDTH) for d in DILATIONS]
                 + [_stream_rows(d, tm, LANES) for d in DILATIONS]
                 + [_rows(tm, B_WIDTH), _resident(expand.shape), _resident(w_out.shape)],
        out_specs=_rows(tm, D),
        out_shape=jax.ShapeDtypeStruct((B, S, D), F32),
        scratch_shapes=[pltpu.VMEM((nd, A_WIDTH // LANES, tm, LANES), F32),
                        pltpu.VMEM((nd, 1, tm, LANES), F32)],
        compiler_params=_params("parallel", "parallel"),
        name="even_outproj",
    )(x, *os, *lses, b, expand, w_out)


def _ffn_kernel(x_ref, g_ref, wup_ref, cw_ref, wdn_ref, o_ref, h_ref, halo_ref):
    @pl.when(pl.program_id(1) == 0)
    def _():
        halo_ref[...] = jnp.zeros_like(halo_ref)

    x = x_ref[...]
    tm = x.shape[0]
    xn = _rms(x, g_ref[...]).astype(BF16)

    def conv_up(col):
        cols = slice(col, col + FF_TILE)
        u = jnp.dot(xn, wup_ref[:, cols], preferred_element_type=F32)
        y = _causal_conv3(u, halo_ref[:, cols], cw_ref[:, cols])
        halo_ref[:, cols] = u[tm - SUBLANES:]
        return y

    for c in range(D_FF // FF_TILE):
        gate = conv_up(c * FF_TILE)
        val = conv_up(D_FF + c * FF_TILE)
        h_ref[:, c * FF_TILE:(c + 1) * FF_TILE] = (_silu(gate) * val).astype(BF16)
    o_ref[...] = x + jnp.dot(h_ref[...], wdn_ref[...], preferred_element_type=F32)


def _ffn(x, g, w_up, conv_w, w_down):
    B, S, D = x.shape
    tm = FFN_ROW_TILE
    return pl.pallas_call(
        _ffn_kernel,
        grid=(B, S // tm),
        in_specs=[_rows(tm, D), _resident((1, D)), _resident(w_up.shape), _resident(conv_w.shape),
                  _resident(w_down.shape)],
        out_specs=_rows(tm, D),
        out_shape=jax.ShapeDtypeStruct((B, S, D), F32),
        scratch_shapes=[pltpu.VMEM((tm, D_FF), BF16), pltpu.VMEM((SUBLANES, 2 * D_FF), F32)],
        compiler_params=_params("parallel", "arbitrary"),
        name="conv_glu_ffn",
    )(x, g, w_up, conv_w, w_down)


def _rope_table_kernel(cos_ref, sin_ref):
    half = cos_ref.shape[1]
    pos = lax.broadcasted_iota(jnp.int32, cos_ref.shape, 0).astype(F32)
    j = lax.broadcasted_iota(jnp.int32, cos_ref.shape, 1).astype(F32)
    inv = jnp.exp(j * (-math.log(ROPE_BASE) / half))
    ang = pos * inv
    cos_ref[...] = jnp.cos(ang)
    sin_ref[...] = jnp.sin(ang)


def _rope_table(S):
    half = RET_KDIM // 2
    shape = jax.ShapeDtypeStruct((S, half), F32)
    return pl.pallas_call(_rope_table_kernel, out_shape=[shape, shape], name="rope_table")()


def _retproj_kernel(x_ref, g_ref, wq_ref, wk_ref, wv_ref, wg_ref, cos_ref, sin_ref,
                    q_ref, k_ref, v_ref, gate_ref):
    xn = _rms(x_ref[...], g_ref[...]).astype(BF16)
    cos = cos_ref[...]
    sin = sin_ref[...]
    half = RET_KDIM // 2

    def rotary_store(t, dst_ref):
        for h in range(RET_HEADS):
            x1 = t[:, h * RET_KDIM:h * RET_KDIM + half]
            x2 = t[:, h * RET_KDIM + half:(h + 1) * RET_KDIM]
            dst_ref[:, h * RET_KDIM:h * RET_KDIM + half] = (x1 * cos - x2 * sin).astype(BF16)
            dst_ref[:, h * RET_KDIM + half:(h + 1) * RET_KDIM] = (x1 * sin + x2 * cos).astype(BF16)

    rotary_store(jnp.dot(xn, wq_ref[...], preferred_element_type=F32), q_ref)
    rotary_store(jnp.dot(xn, wk_ref[...], preferred_element_type=F32) * (RET_KDIM ** -0.5), k_ref)
    v_ref[...] = jnp.dot(xn, wv_ref[...], preferred_element_type=F32).astype(BF16)
    gate_ref[...] = _silu(jnp.dot(xn, wg_ref[...], preferred_element_type=F32)).astype(BF16)


def _retproj(x, g, wq, wk, wv, wg, cos, sin):
    B, S, D = x.shape
    tm = ROW_TILE
    kw, vw = RET_HEADS * RET_KDIM, RET_HEADS * RET_VDIM
    table = pl.BlockSpec((tm, RET_KDIM // 2), lambda b, i: (i, 0))
    return pl.pallas_call(
        _retproj_kernel,
        grid=(B, S // tm),
        in_specs=[_rows(tm, D), _resident((1, D)), _resident(wq.shape), _resident(wk.shape),
                  _resident(wv.shape), _resident(wg.shape), table, table],
        out_specs=[_rows(tm, kw), _rows(tm, kw), _rows(tm, vw), _rows(tm, vw)],
        out_shape=[jax.ShapeDtypeStruct((B, S, kw), BF16), jax.ShapeDtypeStruct((B, S, kw), BF16),
                   jax.ShapeDtypeStruct((B, S, vw), BF16), jax.ShapeDtypeStruct((B, S, vw), BF16)],
        compiler_params=_params("parallel", "parallel"),
        name="ret_proj",
    )(x, g, wq, wk, wv, wg, cos, sin)


def _retention_kernel(q_ref, k_ref, v_ref, gate_ref, gn_ref, o_ref, state_ref):
    @pl.when(pl.program_id(1) == 0)
    def _():
        state_ref[...] = jnp.zeros_like(state_ref)

    C = RET_CHUNK
    ii = lax.broadcasted_iota(jnp.int32, (C, C), 0)
    jj = lax.broadcasted_iota(jnp.int32, (C, C), 1)
    diff = (ii - jj).astype(F32)
    pos = lax.broadcasted_iota(jnp.int32, (C, 1), 0).astype(F32)

    for h in range(RET_HEADS):
        log_g = math.log1p(-(2.0 ** (-5.0 - h)))
        inner_decay = jnp.where(diff >= 0, jnp.exp(log_g * jnp.maximum(diff, 0.0)), 0.0)
        q_decay = jnp.exp(log_g * (pos + 1.0))
        k_decay = jnp.exp(log_g * (C - 1.0 - pos))
        chunk_decay = math.exp(log_g * C)
        kcols = slice(h * RET_KDIM, (h + 1) * RET_KDIM)
        vcols = slice(h * RET_VDIM, (h + 1) * RET_VDIM)
        for c in range(q_ref.shape[0] // C):
            rows = slice(c * C, (c + 1) * C)
            qc = q_ref[rows, kcols]
            kc = k_ref[rows, kcols]
            vc = v_ref[rows, vcols]
            state = state_ref[h]
            scores = lax.dot_general(qc, kc, (((1,), (1,)), ((), ())),
                                     preferred_element_type=F32) * inner_decay
            o = (jnp.dot(scores.astype(BF16), vc, preferred_element_type=F32)
                 + jnp.dot(qc, state.astype(BF16), preferred_element_type=F32) * q_decay)
            kd = (kc.astype(F32) * k_decay).astype(BF16)
            state_ref[h] = state * chunk_decay + lax.dot_general(
                kd, vc, (((0,), (0,)), ((), ())), preferred_element_type=F32)
            mu = jnp.mean(o, axis=-1, keepdims=True)
            var = jnp.mean(jnp.square(o - mu), axis=-1, keepdims=True)
            on = (o - mu) * lax.rsqrt(var + EPS) * gn_ref[h:h + 1, :]
            o_ref[rows, vcols] = (gate_ref[rows, vcols].astype(F32) * on).astype(BF16)


def _retention(q, k, v, gate, gn_gain):
    B, S, kw = q.shape
    vw = v.shape[-1]
    tc = RET_TILE
    return pl.pallas_call(
        _retention_kernel,
        grid=(B, S // tc),
        in_specs=[_rows(tc, kw), _rows(tc, kw), _rows(tc, vw), _rows(tc, vw),
                  _resident(gn_gain.shape)],
        out_specs=_rows(tc, vw),
        out_shape=jax.ShapeDtypeStruct((B, S, vw), BF16),
        scratch_shapes=[pltpu.VMEM((RET_HEADS, RET_KDIM, RET_VDIM), F32)],
        compiler_params=_params("parallel", "arbitrary"),
        name="retention",
    )(q, k, v, gate, gn_gain)


def _proj_residual_kernel(x_ref, a_ref, w_ref, o_ref):
    o_ref[...] = x_ref[...] + jnp.dot(a_ref[...], w_ref[...], preferred_element_type=F32)


def _proj_residual(x, a, w):
    B, S, D = x.shape
    tm = ROW_TILE
    return pl.pallas_call(
        _proj_residual_kernel,
        grid=(B, S // tm),
        in_specs=[_rows(tm, D), _rows(tm, a.shape[-1]), _resident(w.shape)],
        out_specs=_rows(tm, D),
        out_shape=jax.ShapeDtypeStruct((B, S, D), F32),
        compiler_params=_params("parallel", "parallel"),
        name="ret_outproj",
    )(x, a, w)


def _head_sum_matrix():
    g = jnp.arange(A_WIDTH // 2) // A_HEAD_DIM
    return (g[:, None] == g[None, :]).astype(BF16)


def _head_expand_matrix():
    head_of_lane = jnp.arange(A_WIDTH) // A_HEAD_DIM
    return (jnp.arange(LANES)[:, None] == head_of_lane[None, :]).astype(BF16)


def kernel(x, even_norm, even_w_in, even_q_gain, even_k_gain, even_sconv_w, even_w_out, odd_norm,
           ret_wq, ret_wk, ret_wv, ret_wg, ret_gn_gain, ret_wo, ffn_norm, ffn_w_up, ffn_conv_w,
           ffn_w_down):
    B, S, D = x.shape
    depth = ffn_norm.shape[0]
    bf = lambda w: w.astype(BF16)
    row = lambda v: v.reshape(1, -1).astype(F32)
    hsum = _head_sum_matrix()
    expand = _head_expand_matrix()
    cos = sin = None

    for l in range(depth):
        if l % 2 == 0:
            e = l // 2
            qs, ks, vs, b = _inproj(x, row(even_norm[e]), bf(even_w_in[e]),
                                    row(jnp.tile(even_q_gain[e], A_HEADS) * _Q_SCALE),
                                    row(jnp.tile(even_k_gain[e], A_HEADS)),
                                    even_sconv_w[e], hsum)
            os, lses = zip(*[_dilated_attention_one(q, k, v, d)
                             for q, k, v, d in zip(qs, ks, vs, DILATIONS)])
            x = _outproj(x, os, lses, b, expand, bf(even_w_out[e]))
        else:
            o = l // 2
            if cos is None:
                cos, sin = _rope_table(S)
            q, k, v, gate = _retproj(x, row(odd_norm[o]), bf(ret_wq[o]), bf(ret_wk[o]),
                                     bf(ret_wv[o]), bf(ret_wg[o]), cos, sin)
            a = _retention(q, k, v, gate, ret_gn_gain[o])
            x = _proj_residual(x, a, bf(ret_wo[o]))
        x = _ffn(x, row(ffn_norm[l]), bf(ffn_w_up[l]), ffn_conv_w[l], bf(ffn_w_down[l]))
    return x
```

```python
import functools
import math

import jax
import jax.numpy as jnp
from jax import lax
from jax.experimental import pallas as pl
from jax.experimental.pallas import tpu as pltpu

F32 = jnp.float32
BF16 = jnp.bfloat16

D_MODEL = 1024
A_HEADS = 8
A_HEAD_DIM = 64
A_WIDTH = A_HEADS * A_HEAD_DIM
B_WIDTH = D_MODEL - A_WIDTH
DILATIONS = (1, 4, 16)
REACH = 128
RET_HEADS = 4
RET_KDIM = D_MODEL // RET_HEADS
RET_VDIM = 2 * D_MODEL // RET_HEADS
RET_CHUNK = 256
ROPE_BASE = 10000.0
D_FF = 2816
EPS = 1e-6

SUBLANES = 8
LANES = 128
VMEM_LIMIT_BYTES = 56 * 1024 * 1024

ROW_TILE = 512
FFN_ROW_TILE = 1024
FF_TILE = 256
ATTN_ROWS = 1024
RET_TILE = 256


def _params(*semantics):
    return pltpu.CompilerParams(dimension_semantics=semantics, vmem_limit_bytes=VMEM_LIMIT_BYTES)


def _resident(shape):
    nd = len(shape)
    return pl.BlockSpec(shape, lambda *_: (0,) * nd, pipeline_mode=pl.Buffered(1))


def _rows(tm, width):
    return pl.BlockSpec((None, tm, width), lambda b, i: (b, i, 0))


def _stream_rows(dil, tm, width):
    return pl.BlockSpec((None, dil, tm // dil, width), lambda b, i: (b, 0, i, 0))


def _rms(x, g):
    ms = jnp.mean(x * x, axis=-1, keepdims=True)
    return x * lax.rsqrt(ms + EPS) * g


def _silu(g):
    return g / (1.0 + jnp.exp2(g * -math.log2(math.e)))


def _shift_rows(cur, prev, k):
    rolled = pltpu.roll(cur, k, axis=0)
    prev_rolled = pltpu.roll(prev, k, axis=0)
    row = lax.broadcasted_iota(jnp.int32, prev.shape, 0)
    head = jnp.where(row < k, prev_rolled, rolled[:SUBLANES])
    return jnp.concatenate([head, rolled[SUBLANES:]], axis=0)


def _causal_conv3(cur, prev, w):
    return (_shift_rows(cur, prev, 2) * w[0:1] + _shift_rows(cur, prev, 1) * w[1:2]
            + cur * w[2:3])


def _stream_group(dil):
    return max(LANES, 2 * SUBLANES * dil)


def _stream_perm_matrix(dil):
    group = _stream_group(dil)
    dst = jnp.arange(group)
    src = (dst % (group // dil)) * dil + dst // (group // dil)
    return (jnp.arange(group)[None, :] == src[:, None]).astype(BF16)


def _store_streams(t, perm_refs, out_refs):
    tm = t.shape[0]
    tb = t.astype(BF16)
    for dil, perm_ref, out_ref in zip(DILATIONS, perm_refs, out_refs):
        if dil == 1:
            out_ref[0] = tb
            continue
        group = perm_ref.shape[0]
        n = group // dil
        for g in range(tm // group):
            y = jnp.dot(perm_ref[...], tb[g * group:(g + 1) * group, :],
                        preferred_element_type=F32).astype(BF16)
            for r in range(dil):
                out_ref[r, g * n:(g + 1) * n, :] = y[r * n:(r + 1) * n, :]


def _inproj_kernel(x_ref, g_ref, w_ref, qg_ref, kg_ref, cw_ref, hsum_ref, *rest):
    nd = len(DILATIONS)
    perm_refs, rest = rest[:nd], rest[nd:]
    q_refs, k_refs, v_refs = rest[:nd], rest[nd:2 * nd], rest[2 * nd:3 * nd]
    b_ref, halo_ref = rest[3 * nd:]

    @pl.when(pl.program_id(1) == 0)
    def _():
        halo_ref[...] = jnp.zeros_like(halo_ref)

    xn = _rms(x_ref[...], g_ref[...]).astype(BF16)

    def proj(idx):
        return jnp.dot(xn, w_ref[:, idx * A_WIDTH:(idx + 1) * A_WIDTH], preferred_element_type=F32)

    def head_norm(t, gain):
        t2 = (t * t).astype(BF16)
        half = A_WIDTH // 2
        ss = jnp.concatenate(
            [jnp.dot(t2[:, :half], hsum_ref[...], preferred_element_type=F32),
             jnp.dot(t2[:, half:], hsum_ref[...], preferred_element_type=F32)], axis=1)
        return t * lax.rsqrt(ss * (1.0 / A_HEAD_DIM) + EPS) * gain

    _store_streams(head_norm(proj(0), qg_ref[...]), perm_refs, q_refs)
    _store_streams(head_norm(proj(1), kg_ref[...]), perm_refs, k_refs)
    _store_streams(proj(2), perm_refs, v_refs)
    gate_b = proj(3)
    s = proj(4) * proj(5)
    conv = _causal_conv3(s, halo_ref[...], cw_ref[...])
    halo_ref[...] = s[s.shape[0] - SUBLANES:]
    b_ref[...] = (gate_b * conv).astype(BF16)


def _inproj(x, g, w_in, q_gain, k_gain, sconv_w, hsum):
    B, S, D = x.shape
    tm = ROW_TILE
    streams = [jax.ShapeDtypeStruct((B, d, S // d, A_WIDTH), BF16) for d in DILATIONS]
    stream_specs = [_stream_rows(d, tm, A_WIDTH) for d in DILATIONS]
    perms = [_stream_perm_matrix(d) for d in DILATIONS]
    outs = pl.pallas_call(
        _inproj_kernel,
        grid=(B, S // tm),
        in_specs=[_rows(tm, D), _resident((1, D)), _resident(w_in.shape), _resident((1, A_WIDTH)),
                  _resident((1, A_WIDTH)), _resident(sconv_w.shape), _resident(hsum.shape)]
                 + [_resident(p.shape) for p in perms],
        out_specs=stream_specs * 3 + [_rows(tm, B_WIDTH)],
        out_shape=streams * 3 + [jax.ShapeDtypeStruct((B, S, B_WIDTH), BF16)],
        scratch_shapes=[pltpu.VMEM((SUBLANES, B_WIDTH), F32)],
        compiler_params=_params("parallel", "arbitrary"),
        name="even_inproj",
    )(x, g, w_in, q_gain, k_gain, sconv_w, hsum, *perms)
    nd = len(DILATIONS)
    return outs[:nd], outs[nd:2 * nd], outs[2 * nd:3 * nd], outs[3 * nd]


_Q_SCALE = A_HEAD_DIM ** -0.5 * math.log2(math.e)


def _col_reduce(x, op):
    parts = x.reshape(4, x.shape[0] // 4, x.shape[1])
    part = op(op(parts[0], parts[1]), op(parts[2], parts[3]))
    reduce = jnp.max if op is jnp.maximum else jnp.sum
    return reduce(part, axis=0, keepdims=True)


def _attn_kernel(q_ref, kp_ref, kc_ref, vp_ref, vc_ref, o_ref, lse_ref,
                 qt_scr, k_scr, vwin_scr, bias_scr, s_scr, ot_scr, lse_scr):
    first_tile = pl.program_id(2) == 0
    hd = A_HEAD_DIM
    n_pairs = A_WIDTH // LANES
    n_streams, tq = q_ref.shape[0], q_ref.shape[1]
    blocks_per_stream = tq // REACH
    n_blocks = n_streams * blocks_per_stream
    assert blocks_per_stream & (blocks_per_stream - 1) == 0 and n_blocks % 2 == 0

    kj = lax.broadcasted_iota(jnp.int32, (2 * REACH, 2 * REACH), 0)
    qi = lax.broadcasted_iota(jnp.int32, (2 * REACH, 2 * REACH), 1) % REACH
    band = (kj >= qi) & (kj <= qi + REACH)
    bias_scr[0] = jnp.where(band & ((kj >= REACH) | jnp.logical_not(first_tile)), 0.0, -1e30)
    bias_scr[1] = jnp.where(band, 0.0, -1e30)
    for s in range(n_streams):
        q_t = q_ref[s].T
        v_t = jnp.concatenate([vp_ref[s], vc_ref[s]], axis=0).T
        for j in range(blocks_per_stream):
            qt_scr[s * blocks_per_stream + j] = q_t[:, j * REACH:(j + 1) * REACH]
            vwin_scr[s * blocks_per_stream + j] = v_t[:, j * REACH:(j + 2) * REACH]
        for pr in range(n_pairs):
            slab = slice(pr * LANES, (pr + 1) * LANES)
            k_scr[pr, s, 0:REACH, :] = kp_ref[s, :, slab]
            k_scr[pr, s, REACH:, :] = kc_ref[s, :, slab]
    zeros = jnp.zeros((hd, REACH), BF16)

    def block_scores(blk, slot):
        s = lax.shift_right_logical(blk, blocks_per_stream.bit_length() - 1)
        j = blk & (blocks_per_stream - 1)
        bias = bias_scr[jnp.minimum(j, 1)]
        maxima = []
        for pr in range(n_pairs):
            k_win = k_scr[pr, s, pl.ds(pl.multiple_of(j * REACH, REACH), 2 * REACH), :]
            qt = qt_scr[blk, pr * LANES:(pr + 1) * LANES, :]
            q_bd = jnp.concatenate([jnp.concatenate([qt[:hd], zeros], axis=1),
                                    jnp.concatenate([zeros, qt[hd:]], axis=1)], axis=0)
            sc = jnp.dot(k_win, q_bd, preferred_element_type=F32) + bias
            s_scr[slot, pr] = sc
            maxima.append(_col_reduce(sc, jnp.maximum))
        return tuple(maxima)

    def block_finish(blk, slot, maxima):
        for pr in range(n_pairs):
            p_t = jnp.exp2(s_scr[slot, pr] - maxima[pr])
            den = _col_reduce(p_t, jnp.add)
            v_pair = vwin_scr[blk, pr * LANES:(pr + 1) * LANES, :]
            o_t = jnp.dot(v_pair, p_t.astype(BF16), preferred_element_type=F32) / den
            ot_scr[blk, pr * LANES:(pr + 1) * LANES, :] = jnp.concatenate(
                [o_t[:hd, :REACH], o_t[hd:, REACH:]], axis=0).astype(BF16)
            lse = (maxima[pr] + jnp.log2(den)) * math.log(2.0)
            lse_scr[blk, pr] = jnp.broadcast_to(lse, (SUBLANES, 2 * REACH))

    def two_blocks(u, maxima_even, lookahead=True):
        blk = 2 * u
        maxima_odd = block_scores(blk + 1, 1)
        block_finish(blk, 0, maxima_even)
        maxima_next = block_scores(blk + 2, 0) if lookahead else None
        block_finish(blk + 1, 1, maxima_odd)
        return maxima_next

    maxima = lax.fori_loop(0, n_blocks // 2 - 1, two_blocks, block_scores(jnp.int32(0), 0))
    two_blocks(jnp.int32(n_blocks // 2 - 1), maxima, lookahead=False)

    for blk in range(n_blocks):
        s, j = divmod(blk, blocks_per_stream)
        rows = slice(j * REACH, (j + 1) * REACH)
        o_ref[s, rows, :] = ot_scr[blk].T
        lse_rows = []
        for pr in range(n_pairs):
            lse_rows += [lse_scr[blk, pr, 0:1, :REACH], lse_scr[blk, pr, 0:1, REACH:]]
        lse_rows.append(jnp.zeros((LANES - A_HEADS, REACH), F32))
        lse_ref[s, rows, :] = jnp.concatenate(lse_rows, axis=0).T


def _dilated_attention_one(q, k, v, dil):
    B, _, L, W = q.shape
    tq = min(ATTN_ROWS, L)
    ns = min(ATTN_ROWS // tq, dil)
    n_blocks, n_pairs = ns * tq // REACH, W // LANES
    cur = pl.BlockSpec((None, ns, tq, W), lambda b, r, i: (b, r, i, 0))
    prev = pl.BlockSpec((None, ns, REACH, W),
                        lambda b, r, i: (b, r, jnp.maximum(i * (tq // REACH) - 1, 0), 0))
    return pl.pallas_call(
        _attn_kernel,
        grid=(B, dil // ns, L // tq),
        in_specs=[cur, prev, cur, prev, cur],
        out_specs=[cur, pl.BlockSpec((None, ns, tq, LANES), lambda b, r, i: (b, r, i, 0))],
        out_shape=[jax.ShapeDtypeStruct((B, dil, L, W), BF16),
                   jax.ShapeDtypeStruct((B, dil, L, LANES), F32)],
        scratch_shapes=[
            pltpu.VMEM((n_blocks, W, REACH), BF16),
            pltpu.VMEM((n_pairs, ns, REACH + tq, LANES), BF16),
            pltpu.VMEM((n_blocks, W, 2 * REACH), BF16),
            pltpu.VMEM((2, 2 * REACH, 2 * REACH), F32),
            pltpu.VMEM((2, n_pairs, 2 * REACH, 2 * REACH), F32),
            pltpu.VMEM((n_blocks, W, REACH), BF16),
            pltpu.VMEM((n_blocks, n_pairs, SUBLANES, 2 * REACH), F32),
        ],
        compiler_params=_params("parallel", "parallel", "arbitrary"),
        name=f"dilated_attn_d{dil}",
    )(q, k, k, v, v)


def _load_streams(src_ref, dil, slab_ref):
    n_slabs = src_ref.shape[-1] // LANES
    if dil == 1:
        t = src_ref[0].astype(F32)
        return [t[:, s * LANES:(s + 1) * LANES] for s in range(n_slabs)]
    n = src_ref.shape[1]
    for r in range(dil):
        for s in range(n_slabs):
            slab_ref[s, pl.ds(r, n, stride=dil), :] = (
                src_ref[r, :, s * LANES:(s + 1) * LANES].astype(F32))
    return [slab_ref[s] for s in range(n_slabs)]


def _outproj_kernel(x_ref, *rest):
    nd = len(DILATIONS)
    o_refs, l_refs = rest[:nd], rest[nd:2 * nd]
    b_ref, expand_ref, w_ref, out_ref, o_slab_ref, l_slab_ref = rest[2 * nd:]
    n_slabs = A_WIDTH // LANES

    ls = [_load_streams(l_ref, d, l_slab_ref.at[n])[0]
          for n, (d, l_ref) in enumerate(zip(DILATIONS, l_refs))]
    mx = functools.reduce(jnp.maximum, ls)
    es = [jnp.exp(l - mx) for l in ls]
    tot = functools.reduce(jnp.add, es)
    a_slabs = [None] * n_slabs
    for n, (d, e, o_ref) in enumerate(zip(DILATIONS, es, o_refs)):
        w = e / tot
        hi = w.astype(BF16)
        lo = (w - hi.astype(F32)).astype(BF16)
        wf = (jnp.dot(hi, expand_ref[...], preferred_element_type=F32)
              + jnp.dot(lo, expand_ref[...], preferred_element_type=F32))
        o_slabs = _load_streams(o_ref, d, o_slab_ref.at[n])
        for s in range(n_slabs):
            term = wf[:, s * LANES:(s + 1) * LANES] * o_slabs[s]
            a_slabs[s] = term if a_slabs[s] is None else a_slabs[s] + term
    a = jnp.concatenate(a_slabs, axis=1).astype(BF16)
    y = (jnp.dot(a, w_ref[:A_WIDTH, :], preferred_element_type=F32)
         + jnp.dot(b_ref[...], w_ref[A_WIDTH:, :], preferred_element_type=F32))
    out_ref[...] = x_ref[...] + y


def _outproj(x, os, lses, b, expand, w_out):
    B, S, D = x.shape
    tm = ROW_TILE
    nd = len(DILATIONS)
    return pl.pallas_call(
        _outproj_kernel,
        grid=(B, S // tm),
        in_specs=[_rows(tm, D)] + [_stream_rows(d, tm, A_WIDTH) for d in DILATIONS]
                 + [_stream_rows(d, tm, LANES) for d in DILATIONS]
                 + [_rows(tm, B_WIDTH), _resident(expand.shape), _resident(w_out.shape)],
        out_specs=_rows(tm, D),
        out_shape=jax.ShapeDtypeStruct((B, S, D), F32),
        scratch_shapes=[pltpu.VMEM((nd, A_WIDTH // LANES, tm, LANES), F32),
                        pltpu.VMEM((nd, 1, tm, LANES), F32)],
        compiler_params=_params("parallel", "parallel"),
        name="even_outproj",
    )(x, *os, *lses, b, expand, w_out)


def _ffn_kernel(x_ref, g_ref, wup_ref, cw_ref, wdn_ref, o_ref, h_ref, halo_ref):
    @pl.when(pl.program_id(1) == 0)
    def _():
        halo_ref[...] = jnp.zeros_like(halo_ref)

    x = x_ref[...]
    tm = x.shape[0]
    xn = _rms(x, g_ref[...]).astype(BF16)

    def conv_up(col):
        cols = slice(col, col + FF_TILE)
        u = jnp.dot(xn, wup_ref[:, cols], preferred_element_type=F32)
        y = _causal_conv3(u, halo_ref[:, cols], cw_ref[:, cols])
        halo_ref[:, cols] = u[tm - SUBLANES:]
        return y

    for c in range(D_FF // FF_TILE):
        gate = conv_up(c * FF_TILE)
        val = conv_up(D_FF + c * FF_TILE)
        h_ref[:, c * FF_TILE:(c + 1) * FF_TILE] = (_silu(gate) * val).astype(BF16)
    o_ref[...] = x + jnp.dot(h_ref[...], wdn_ref[...], preferred_element_type=F32)


def _ffn(x, g, w_up, conv_w, w_down):
    B, S, D = x.shape
    tm = FFN_ROW_TILE
    return pl.pallas_call(
        _ffn_kernel,
        grid=(B, S // tm),
        in_specs=[_rows(tm, D), _resident((1, D)), _resident(w_up.shape), _resident(conv_w.shape),
                  _resident(w_down.shape)],
        out_specs=_rows(tm, D),
        out_shape=jax.ShapeDtypeStruct((B, S, D), F32),
        scratch_shapes=[pltpu.VMEM((tm, D_FF), BF16), pltpu.VMEM((SUBLANES, 2 * D_FF), F32)],
        compiler_params=_params("parallel", "arbitrary"),
        name="conv_glu_ffn",
    )(x, g, w_up, conv_w, w_down)


def _rope_table_kernel(cos_ref, sin_ref):
    half = cos_ref.shape[1]
    pos = lax.broadcasted_iota(jnp.int32, cos_ref.shape, 0).astype(F32)
    j = lax.broadcasted_iota(jnp.int32, cos_ref.shape, 1).astype(F32)
    inv = jnp.exp(j * (-math.log(ROPE_BASE) / half))
    ang = pos * inv
    cos_ref[...] = jnp.cos(ang)
    sin_ref[...] = jnp.sin(ang)


def _rope_table(S):
    half = RET_KDIM // 2
    shape = jax.ShapeDtypeStruct((S, half), F32)
    return pl.pallas_call(_rope_table_kernel, out_shape=[shape, shape], name="rope_table")()


def _retproj_kernel(x_ref, g_ref, wq_ref, wk_ref, wv_ref, wg_ref, cos_ref, sin_ref,
                    q_ref, k_ref, v_ref, gate_ref):
    xn = _rms(x_ref[...], g_ref[...]).astype(BF16)
    cos = cos_ref[...]
    sin = sin_ref[...]
    half = RET_KDIM // 2

    def rotary_store(t, dst_ref):
        for h in range(RET_HEADS):
            x1 = t[:, h * RET_KDIM:h * RET_KDIM + half]
            x2 = t[:, h * RET_KDIM + half:(h + 1) * RET_KDIM]
            dst_ref[:, h * RET_KDIM:h * RET_KDIM + half] = (x1 * cos - x2 * sin).astype(BF16)
            dst_ref[:, h * RET_KDIM + half:(h + 1) * RET_KDIM] = (x1 * sin + x2 * cos).astype(BF16)

    rotary_store(jnp.dot(xn, wq_ref[...], preferred_element_type=F32), q_ref)
    rotary_store(jnp.dot(xn, wk_ref[...], preferred_element_type=F32) * (RET_KDIM ** -0.5), k_ref)
    v_ref[...] = jnp.dot(xn, wv_ref[...], preferred_element_type=F32).astype(BF16)
    gate_ref[...] = _silu(jnp.dot(xn, wg_ref[...], preferred_element_type=F32)).astype(BF16)


def _retproj(x, g, wq, wk, wv, wg, cos, sin):
    B, S, D = x.shape
    tm = ROW_TILE
    kw, vw = RET_HEADS * RET_KDIM, RET_HEADS * RET_VDIM
    table = pl.BlockSpec((tm, RET_KDIM // 2), lambda b, i: (i, 0))
    return pl.pallas_call(
        _retproj_kernel,
        grid=(B, S // tm),
        in_specs=[_rows(tm, D), _resident((1, D)), _resident(wq.shape), _resident(wk.shape),
                  _resident(wv.shape), _resident(wg.shape), table, table],
        out_specs=[_rows(tm, kw), _rows(tm, kw), _rows(tm, vw), _rows(tm, vw)],
        out_shape=[jax.ShapeDtypeStruct((B, S, kw), BF16), jax.ShapeDtypeStruct((B, S, kw), BF16),
                   jax.ShapeDtypeStruct((B, S, vw), BF16), jax.ShapeDtypeStruct((B, S, vw), BF16)],
        compiler_params=_params("parallel", "parallel"),
        name="ret_proj",
    )(x, g, wq, wk, wv, wg, cos, sin)


def _retention_kernel(q_ref, k_ref, v_ref, gate_ref, o_ref, state_ref, decay_ref):
    C = RET_CHUNK
    log_gs = [math.log1p(-(2.0 ** (-5.0 - h))) for h in range(RET_HEADS)]

    @pl.when((pl.program_id(0) == 0) & (pl.program_id(1) == 0))
    def _():
        ii = lax.broadcasted_iota(jnp.int32, (C, C), 0)
        jj = lax.broadcasted_iota(jnp.int32, (C, C), 1)
        diff = (ii - jj).astype(F32)
        for h in range(RET_HEADS):
            decay_ref[h] = jnp.where(diff >= 0, jnp.exp(log_gs[h] * jnp.maximum(diff, 0.0)), 0.0)

    @pl.when(pl.program_id(1) == 0)
    def _():
        state_ref[...] = jnp.zeros_like(state_ref)

    pos = lax.broadcasted_iota(jnp.int32, (C, 1), 0).astype(F32)

    for h in range(RET_HEADS):
        log_g = log_gs[h]
        inner_decay = decay_ref[h]
        q_decay = jnp.exp(log_g * (pos + 1.0))
        k_decay = jnp.exp(log_g * (C - 1.0 - pos))
        chunk_decay = math.exp(log_g * C)
        kcols = slice(h * RET_KDIM, (h + 1) * RET_KDIM)
        vcols = slice(h * RET_VDIM, (h + 1) * RET_VDIM)
        for c in range(q_ref.shape[0] // C):
            rows = slice(c * C, (c + 1) * C)
            qc = q_ref[rows, kcols]
            kc = k_ref[rows, kcols]
            vc = v_ref[rows, vcols]
            state = state_ref[h]
            scores = lax.dot_general(qc, kc, (((1,), (1,)), ((), ())),
                                     preferred_element_type=F32) * inner_decay
            o = (jnp.dot(scores.astype(BF16), vc, preferred_element_type=F32)
                 + jnp.dot(qc, state.astype(BF16), preferred_element_type=F32) * q_decay)
            kd = (kc.astype(F32) * k_decay).astype(BF16)
            state_ref[h] = state * chunk_decay + lax.dot_general(
                kd, vc, (((0,), (0,)), ((), ())), preferred_element_type=F32)
            mu = jnp.mean(o, axis=-1, keepdims=True)
            var = jnp.mean(jnp.square(o - mu), axis=-1, keepdims=True)
            on = (o - mu) * lax.rsqrt(var + EPS)
            o_ref[rows, vcols] = (gate_ref[rows, vcols].astype(F32) * on).astype(BF16)


def _retention(q, k, v, gate):
    B, S, kw = q.shape
    vw = v.shape[-1]
    tc = RET_TILE
    return pl.pallas_call(
        _retention_kernel,
        grid=(B, S // tc),
        in_specs=[_rows(tc, kw), _rows(tc, kw), _rows(tc, vw), _rows(tc, vw)],
        out_specs=_rows(tc, vw),
        out_shape=jax.ShapeDtypeStruct((B, S, vw), BF16),
        scratch_shapes=[pltpu.VMEM((RET_HEADS, RET_KDIM, RET_VDIM), F32),
                        pltpu.VMEM((RET_HEADS, RET_CHUNK, RET_CHUNK), F32)],
        compiler_params=_params("arbitrary", "arbitrary"),
        name="retention",
    )(q, k, v, gate)


def _proj_residual_kernel(x_ref, a_ref, w_ref, o_ref):
    o_ref[...] = x_ref[...] + jnp.dot(a_ref[...], w_ref[...], preferred_element_type=F32)


def _proj_residual(x, a, w):
    B, S, D = x.shape
    tm = ROW_TILE
    return pl.pallas_call(
        _proj_residual_kernel,
        grid=(B, S // tm),
        in_specs=[_rows(tm, D), _rows(tm, a.shape[-1]), _resident(w.shape)],
        out_specs=_rows(tm, D),
        out_shape=jax.ShapeDtypeStruct((B, S, D), F32),
        compiler_params=_params("parallel", "parallel"),
        name="ret_outproj",
    )(x, a, w)


def _head_sum_matrix():
    g = jnp.arange(A_WIDTH // 2) // A_HEAD_DIM
    return (g[:, None] == g[None, :]).astype(BF16)


def _head_expand_matrix():
    head_of_lane = jnp.arange(A_WIDTH) // A_HEAD_DIM
    return (jnp.arange(LANES)[:, None] == head_of_lane[None, :]).astype(BF16)


def kernel(x, even_norm, even_w_in, even_q_gain, even_k_gain, even_sconv_w, even_w_out, odd_norm,
           ret_wq, ret_wk, ret_wv, ret_wg, ret_gn_gain, ret_wo, ffn_norm, ffn_w_up, ffn_conv_w,
           ffn_w_down):
    B, S, D = x.shape
    depth = ffn_norm.shape[0]
    bf = lambda w: w.astype(BF16)
    row = lambda v: v.reshape(1, -1).astype(F32)
    hsum = _head_sum_matrix()
    expand = _head_expand_matrix()
    cos = sin = None

    for l in range(depth):
        if l % 2 == 0:
            e = l // 2
            qs, ks, vs, b = _inproj(x, row(even_norm[e]), bf(even_w_in[e]),
                                    row(jnp.tile(even_q_gain[e], A_HEADS) * _Q_SCALE),
                                    row(jnp.tile(even_k_gain[e], A_HEADS)),
                                    even_sconv_w[e], hsum)
            os, lses = zip(*[_dilated_attention_one(q, k, v, d)
                             for q, k, v, d in zip(qs, ks, vs, DILATIONS)])
            x = _outproj(x, os, lses, b, expand, bf(even_w_out[e]))
        else:
            o = l // 2
            if cos is None:
                cos, sin = _rope_table(S)
            q, k, v, gate = _retproj(x, row(odd_norm[o]), bf(ret_wq[o]), bf(ret_wk[o]),
                                     bf(ret_wv[o]), bf(ret_wg[o]), cos, sin)
            a = _retention(q, k, v, gate)
            x = _proj_residual(x, a, bf(ret_gn_gain[o].reshape(-1, 1) * ret_wo[o]))
        x = _ffn(x, row(ffn_norm[l]), bf(ffn_w_up[l]), ffn_conv_w[l], bf(ffn_w_down[l]))
    return x
```

```python
import functools
import math

import jax
import jax.numpy as jnp
from jax import lax
from jax.experimental import pallas as pl
from jax.experimental.pallas import tpu as pltpu

F32 = jnp.float32
BF16 = jnp.bfloat16

D_MODEL = 1024
A_HEADS = 8
A_HEAD_DIM = 64
A_WIDTH = A_HEADS * A_HEAD_DIM
B_WIDTH = D_MODEL - A_WIDTH
DILATIONS = (1, 4, 16)
REACH = 128
RET_HEADS = 4
RET_KDIM = D_MODEL // RET_HEADS
RET_VDIM = 2 * D_MODEL // RET_HEADS
RET_CHUNK = 256
ROPE_BASE = 10000.0
D_FF = 2816
EPS = 1e-6

SUBLANES = 8
LANES = 128
VMEM_LIMIT_BYTES = 56 * 1024 * 1024

ROW_TILE = 1024
FFN_ROW_TILE = 1024
FF_TILE = 256
ATTN_ROWS = 1024
RET_TILE = 512


def _params(*semantics):
    return pltpu.CompilerParams(dimension_semantics=semantics, vmem_limit_bytes=VMEM_LIMIT_BYTES)


def _resident(shape):
    nd = len(shape)
    return pl.BlockSpec(shape, lambda *_: (0,) * nd, pipeline_mode=pl.Buffered(1))


def _rows(tm, width):
    return pl.BlockSpec((None, tm, width), lambda b, i: (b, i, 0))


def _stream_rows(dil, tm, width):
    return pl.BlockSpec((None, dil, tm // dil, width), lambda b, i: (b, 0, i, 0))


def _rms(x, g):
    ms = jnp.mean(x * x, axis=-1, keepdims=True)
    return x * lax.rsqrt(ms + EPS) * g


def _silu(g):
    return g / (1.0 + jnp.exp2(g * -math.log2(math.e)))


def _shift_rows(cur, prev, k):
    rolled = pltpu.roll(cur, k, axis=0)
    prev_rolled = pltpu.roll(prev, k, axis=0)
    row = lax.broadcasted_iota(jnp.int32, prev.shape, 0)
    head = jnp.where(row < k, prev_rolled, rolled[:SUBLANES])
    return jnp.concatenate([head, rolled[SUBLANES:]], axis=0)


def _causal_conv3(cur, prev, w):
    return (_shift_rows(cur, prev, 2) * w[0:1] + _shift_rows(cur, prev, 1) * w[1:2]
            + cur * w[2:3])


def _stream_group(dil):
    return max(LANES, 2 * SUBLANES * dil)


def _stream_perm_matrix(dil):
    group = _stream_group(dil)
    dst = jnp.arange(group)
    src = (dst % (group // dil)) * dil + dst // (group // dil)
    return (jnp.arange(group)[None, :] == src[:, None]).astype(BF16)


def _store_streams(t, perm_refs, out_refs):
    tm = t.shape[0]
    tb = t.astype(BF16)
    for dil, perm_ref, out_ref in zip(DILATIONS, perm_refs, out_refs):
        if dil == 1:
            out_ref[0] = tb
            continue
        group = perm_ref.shape[0]
        n = group // dil
        for g in range(tm // group):
            y = jnp.dot(perm_ref[...], tb[g * group:(g + 1) * group, :],
                        preferred_element_type=F32).astype(BF16)
            for r in range(dil):
                out_ref[r, g * n:(g + 1) * n, :] = y[r * n:(r + 1) * n, :]


def _inproj_kernel(x_ref, g_ref, w_ref, qg_ref, kg_ref, cw_ref, hsum_ref, *rest):
    nd = len(DILATIONS)
    perm_refs, rest = rest[:nd], rest[nd:]
    q_refs, k_refs, v_refs = rest[:nd], rest[nd:2 * nd], rest[2 * nd:3 * nd]
    b_ref, halo_ref = rest[3 * nd:]

    @pl.when(pl.program_id(1) == 0)
    def _():
        halo_ref[...] = jnp.zeros_like(halo_ref)

    xn = _rms(x_ref[...], g_ref[...]).astype(BF16)

    def proj(idx):
        return jnp.dot(xn, w_ref[:, idx * A_WIDTH:(idx + 1) * A_WIDTH], preferred_element_type=F32)

    def head_norm(t, gain):
        t2 = (t * t).astype(BF16)
        half = A_WIDTH // 2
        ss = jnp.concatenate(
            [jnp.dot(t2[:, :half], hsum_ref[...], preferred_element_type=F32),
             jnp.dot(t2[:, half:], hsum_ref[...], preferred_element_type=F32)], axis=1)
        return t * lax.rsqrt(ss * (1.0 / A_HEAD_DIM) + EPS) * gain

    _store_streams(head_norm(proj(0), qg_ref[...]), perm_refs, q_refs)
    _store_streams(head_norm(proj(1), kg_ref[...]), perm_refs, k_refs)
    _store_streams(proj(2), perm_refs, v_refs)
    gate_b = proj(3)
    s = proj(4) * proj(5)
    conv = _causal_conv3(s, halo_ref[...], cw_ref[...])
    halo_ref[...] = s[s.shape[0] - SUBLANES:]
    b_ref[...] = (gate_b * conv).astype(BF16)


def _inproj(x, g, w_in, q_gain, k_gain, sconv_w, hsum):
    B, S, D = x.shape
    tm = ROW_TILE
    streams = [jax.ShapeDtypeStruct((B, d, S // d, A_WIDTH), BF16) for d in DILATIONS]
    stream_specs = [_stream_rows(d, tm, A_WIDTH) for d in DILATIONS]
    perms = [_stream_perm_matrix(d) for d in DILATIONS]
    outs = pl.pallas_call(
        _inproj_kernel,
        grid=(B, S // tm),
        in_specs=[_rows(tm, D), _resident((1, D)), _resident(w_in.shape), _resident((1, A_WIDTH)),
                  _resident((1, A_WIDTH)), _resident(sconv_w.shape), _resident(hsum.shape)]
                 + [_resident(p.shape) for p in perms],
        out_specs=stream_specs * 3 + [_rows(tm, B_WIDTH)],
        out_shape=streams * 3 + [jax.ShapeDtypeStruct((B, S, B_WIDTH), BF16)],
        scratch_shapes=[pltpu.VMEM((SUBLANES, B_WIDTH), F32)],
        compiler_params=_params("parallel", "arbitrary"),
        name="even_inproj",
    )(x, g, w_in, q_gain, k_gain, sconv_w, hsum, *perms)
    nd = len(DILATIONS)
    return outs[:nd], outs[nd:2 * nd], outs[2 * nd:3 * nd], outs[3 * nd]


_Q_SCALE = A_HEAD_DIM ** -0.5 * math.log2(math.e)


def _col_reduce(x, op):
    parts = x.reshape(4, x.shape[0] // 4, x.shape[1])
    part = op(op(parts[0], parts[1]), op(parts[2], parts[3]))
    reduce = jnp.max if op is jnp.maximum else jnp.sum
    return reduce(part, axis=0, keepdims=True)


def _attn_kernel(q_ref, kp_ref, kc_ref, vp_ref, vc_ref, o_ref, lse_ref,
                 qt_scr, k_scr, vwin_scr, bias_scr, s_scr, ot_scr, lse_scr):
    first_tile = pl.program_id(2) == 0
    hd = A_HEAD_DIM
    n_pairs = A_WIDTH // LANES
    n_streams, tq = q_ref.shape[0], q_ref.shape[1]
    blocks_per_stream = tq // REACH
    n_blocks = n_streams * blocks_per_stream
    assert blocks_per_stream & (blocks_per_stream - 1) == 0 and n_blocks % 2 == 0

    kj = lax.broadcasted_iota(jnp.int32, (2 * REACH, 2 * REACH), 0)
    qi = lax.broadcasted_iota(jnp.int32, (2 * REACH, 2 * REACH), 1) % REACH
    band = (kj >= qi) & (kj <= qi + REACH)
    bias_scr[0] = jnp.where(band & ((kj >= REACH) | jnp.logical_not(first_tile)), 0.0, -1e30)
    bias_scr[1] = jnp.where(band, 0.0, -1e30)
    for s in range(n_streams):
        q_t = q_ref[s].T
        v_t = jnp.concatenate([vp_ref[s], vc_ref[s]], axis=0).T
        for j in range(blocks_per_stream):
            qt_scr[s * blocks_per_stream + j] = q_t[:, j * REACH:(j + 1) * REACH]
            vwin_scr[s * blocks_per_stream + j] = v_t[:, j * REACH:(j + 2) * REACH]
        for pr in range(n_pairs):
            slab = slice(pr * LANES, (pr + 1) * LANES)
            k_scr[pr, s, 0:REACH, :] = kp_ref[s, :, slab]
            k_scr[pr, s, REACH:, :] = kc_ref[s, :, slab]
    zeros = jnp.zeros((hd, REACH), BF16)

    def block_scores(blk, slot):
        s = lax.shift_right_logical(blk, blocks_per_stream.bit_length() - 1)
        j = blk & (blocks_per_stream - 1)
        bias = bias_scr[jnp.minimum(j, 1)]
        maxima = []
        for pr in range(n_pairs):
            k_win = k_scr[pr, s, pl.ds(pl.multiple_of(j * REACH, REACH), 2 * REACH), :]
            qt = qt_scr[blk, pr * LANES:(pr + 1) * LANES, :]
            q_bd = jnp.concatenate([jnp.concatenate([qt[:hd], zeros], axis=1),
                                    jnp.concatenate([zeros, qt[hd:]], axis=1)], axis=0)
            sc = jnp.dot(k_win, q_bd, preferred_element_type=F32) + bias
            s_scr[slot, pr] = sc
            maxima.append(_col_reduce(sc, jnp.maximum))
        return tuple(maxima)

    def block_finish(blk, slot, maxima):
        for pr in range(n_pairs):
            p_t = jnp.exp2(s_scr[slot, pr] - maxima[pr])
            den = _col_reduce(p_t, jnp.add)
            v_pair = vwin_scr[blk, pr * LANES:(pr + 1) * LANES, :]
            o_t = jnp.dot(v_pair, p_t.astype(BF16), preferred_element_type=F32) / den
            ot_scr[blk, pr * LANES:(pr + 1) * LANES, :] = jnp.concatenate(
                [o_t[:hd, :REACH], o_t[hd:, REACH:]], axis=0).astype(BF16)
            lse = (maxima[pr] + jnp.log2(den)) * math.log(2.0)
            lse_scr[blk, pr] = jnp.broadcast_to(lse, (SUBLANES, 2 * REACH))

    def two_blocks(u, maxima_even, lookahead=True):
        blk = 2 * u
        maxima_odd = block_scores(blk + 1, 1)
        block_finish(blk, 0, maxima_even)
        maxima_next = block_scores(blk + 2, 0) if lookahead else None
        block_finish(blk + 1, 1, maxima_odd)
        return maxima_next

    maxima = lax.fori_loop(0, n_blocks // 2 - 1, two_blocks, block_scores(jnp.int32(0), 0))
    two_blocks(jnp.int32(n_blocks // 2 - 1), maxima, lookahead=False)

    for blk in range(n_blocks):
        s, j = divmod(blk, blocks_per_stream)
        rows = slice(j * REACH, (j + 1) * REACH)
        o_ref[s, rows, :] = ot_scr[blk].T
        lse_rows = []
        for pr in range(n_pairs):
            lse_rows += [lse_scr[blk, pr, 0:1, :REACH], lse_scr[blk, pr, 0:1, REACH:]]
        lse_rows.append(jnp.zeros((LANES - A_HEADS, REACH), F32))
        lse_ref[s, rows, :] = jnp.concatenate(lse_rows, axis=0).T


def _dilated_attention_one(q, k, v, dil):
    B, _, L, W = q.shape
    tq = min(ATTN_ROWS, L)
    ns = min(ATTN_ROWS // tq, dil)
    n_blocks, n_pairs = ns * tq // REACH, W // LANES
    cur = pl.BlockSpec((None, ns, tq, W), lambda b, r, i: (b, r, i, 0))
    prev = pl.BlockSpec((None, ns, REACH, W),
                        lambda b, r, i: (b, r, jnp.maximum(i * (tq // REACH) - 1, 0), 0))
    return pl.pallas_call(
        _attn_kernel,
        grid=(B, dil // ns, L // tq),
        in_specs=[cur, prev, cur, prev, cur],
        out_specs=[cur, pl.BlockSpec((None, ns, tq, LANES), lambda b, r, i: (b, r, i, 0))],
        out_shape=[jax.ShapeDtypeStruct((B, dil, L, W), BF16),
                   jax.ShapeDtypeStruct((B, dil, L, LANES), F32)],
        scratch_shapes=[
            pltpu.VMEM((n_blocks, W, REACH), BF16),
            pltpu.VMEM((n_pairs, ns, REACH + tq, LANES), BF16),
            pltpu.VMEM((n_blocks, W, 2 * REACH), BF16),
            pltpu.VMEM((2, 2 * REACH, 2 * REACH), F32),
            pltpu.VMEM((2, n_pairs, 2 * REACH, 2 * REACH), F32),
            pltpu.VMEM((n_blocks, W, REACH), BF16),
            pltpu.VMEM((n_blocks, n_pairs, SUBLANES, 2 * REACH), F32),
        ],
        compiler_params=_params("parallel", "parallel", "arbitrary"),
        name=f"dilated_attn_d{dil}",
    )(q, k, k, v, v)


def _load_streams(src_ref, dil, slab_ref):
    n_slabs = src_ref.shape[-1] // LANES
    if dil == 1:
        t = src_ref[0].astype(F32)
        return [t[:, s * LANES:(s + 1) * LANES] for s in range(n_slabs)]
    n = src_ref.shape[1]
    for r in range(dil):
        for s in range(n_slabs):
            slab_ref[s, pl.ds(r, n, stride=dil), :] = (
                src_ref[r, :, s * LANES:(s + 1) * LANES].astype(F32))
    return [slab_ref[s] for s in range(n_slabs)]


def _outproj_kernel(x_ref, *rest):
    nd = len(DILATIONS)
    o_refs, l_refs = rest[:nd], rest[nd:2 * nd]
    b_ref, expand_ref, w_ref, out_ref, o_slab_ref, l_slab_ref = rest[2 * nd:]
    n_slabs = A_WIDTH // LANES

    ls = [_load_streams(l_ref, d, l_slab_ref.at[n])[0]
          for n, (d, l_ref) in enumerate(zip(DILATIONS, l_refs))]
    mx = functools.reduce(jnp.maximum, ls)
    es = [jnp.exp(l - mx) for l in ls]
    tot = functools.reduce(jnp.add, es)
    a_slabs = [None] * n_slabs
    for n, (d, e, o_ref) in enumerate(zip(DILATIONS, es, o_refs)):
        w = e / tot
        hi = w.astype(BF16)
        lo = (w - hi.astype(F32)).astype(BF16)
        wf = (jnp.dot(hi, expand_ref[...], preferred_element_type=F32)
              + jnp.dot(lo, expand_ref[...], preferred_element_type=F32))
        o_slabs = _load_streams(o_ref, d, o_slab_ref.at[n])
        for s in range(n_slabs):
            term = wf[:, s * LANES:(s + 1) * LANES] * o_slabs[s]
            a_slabs[s] = term if a_slabs[s] is None else a_slabs[s] + term
    a = jnp.concatenate(a_slabs, axis=1).astype(BF16)
    y = (jnp.dot(a, w_ref[:A_WIDTH, :], preferred_element_type=F32)
         + jnp.dot(b_ref[...], w_ref[A_WIDTH:, :], preferred_element_type=F32))
    out_ref[...] = x_ref[...] + y


def _outproj(x, os, lses, b, expand, w_out):
    B, S, D = x.shape
    tm = ROW_TILE
    nd = len(DILATIONS)
    return pl.pallas_call(
        _outproj_kernel,
        grid=(B, S // tm),
        in_specs=[_rows(tm, D)] + [_stream_rows(d, tm, A_WIDTH) for d in DILATIONS]
                 + [_stream_rows(d, tm, LANES) for d in DILATIONS]
                 + [_rows(tm, B_WIDTH), _resident(expand.shape), _resident(w_out.shape)],
        out_specs=_rows(tm, D),
        out_shape=jax.ShapeDtypeStruct((B, S, D), F32),
        scratch_shapes=[pltpu.VMEM((nd, A_WIDTH // LANES, tm, LANES), F32),
                        pltpu.VMEM((nd, 1, tm, LANES), F32)],
        compiler_params=_params("parallel", "parallel"),
        name="even_outproj",
    )(x, *os, *lses, b, expand, w_out)


def _ffn_kernel(x_ref, g_ref, wup_ref, cw_ref, wdn_ref, o_ref, h_ref, halo_ref):
    @pl.when(pl.program_id(1) == 0)
    def _():
        halo_ref[...] = jnp.zeros_like(halo_ref)

    x = x_ref[...]
    tm = x.shape[0]
    xn = _rms(x, g_ref[...]).astype(BF16)

    def conv_up(col):
        cols = slice(col, col + FF_TILE)
        u = jnp.dot(xn, wup_ref[:, cols], preferred_element_type=F32)
        y = _causal_conv3(u, halo_ref[:, cols], cw_ref[:, cols])
        halo_ref[:, cols] = u[tm - SUBLANES:]
        return y

    for c in range(D_FF // FF_TILE):
        gate = conv_up(c * FF_TILE)
        val = conv_up(D_FF + c * FF_TILE)
        h_ref[:, c * FF_TILE:(c + 1) * FF_TILE] = (_silu(gate) * val).astype(BF16)
    o_ref[...] = x + jnp.dot(h_ref[...], wdn_ref[...], preferred_element_type=F32)


def _ffn(x, g, w_up, conv_w, w_down):
    B, S, D = x.shape
    tm = FFN_ROW_TILE
    return pl.pallas_call(
        _ffn_kernel,
        grid=(B, S // tm),
        in_specs=[_rows(tm, D), _resident((1, D)), _resident(w_up.shape), _resident(conv_w.shape),
                  _resident(w_down.shape)],
        out_specs=_rows(tm, D),
        out_shape=jax.ShapeDtypeStruct((B, S, D), F32),
        scratch_shapes=[pltpu.VMEM((tm, D_FF), BF16), pltpu.VMEM((SUBLANES, 2 * D_FF), F32)],
        compiler_params=_params("parallel", "arbitrary"),
        name="conv_glu_ffn",
    )(x, g, w_up, conv_w, w_down)


def _rope_table_kernel(cos_ref, sin_ref):
    half = cos_ref.shape[1]
    pos = lax.broadcasted_iota(jnp.int32, cos_ref.shape, 0).astype(F32)
    j = lax.broadcasted_iota(jnp.int32, cos_ref.shape, 1).astype(F32)
    inv = jnp.exp(j * (-math.log(ROPE_BASE) / half))
    ang = pos * inv
    cos_ref[...] = jnp.cos(ang)
    sin_ref[...] = jnp.sin(ang)


def _rope_table(S):
    half = RET_KDIM // 2
    shape = jax.ShapeDtypeStruct((S, half), F32)
    return pl.pallas_call(_rope_table_kernel, out_shape=[shape, shape], name="rope_table")()


def _retproj_kernel(x_ref, g_ref, wq_ref, wk_ref, wv_ref, wg_ref, cos_ref, sin_ref,
                    q_ref, k_ref, v_ref, gate_ref):
    xn = _rms(x_ref[...], g_ref[...]).astype(BF16)
    cos = cos_ref[...]
    sin = sin_ref[...]
    half = RET_KDIM // 2

    def rotary_store(t, dst_ref):
        for h in range(RET_HEADS):
            x1 = t[:, h * RET_KDIM:h * RET_KDIM + half]
            x2 = t[:, h * RET_KDIM + half:(h + 1) * RET_KDIM]
            dst_ref[:, h * RET_KDIM:h * RET_KDIM + half] = (x1 * cos - x2 * sin).astype(BF16)
            dst_ref[:, h * RET_KDIM + half:(h + 1) * RET_KDIM] = (x1 * sin + x2 * cos).astype(BF16)

    rotary_store(jnp.dot(xn, wq_ref[...], preferred_element_type=F32), q_ref)
    rotary_store(jnp.dot(xn, wk_ref[...], preferred_element_type=F32) * (RET_KDIM ** -0.5), k_ref)
    v_ref[...] = jnp.dot(xn, wv_ref[...], preferred_element_type=F32).astype(BF16)
    gate_ref[...] = _silu(jnp.dot(xn, wg_ref[...], preferred_element_type=F32)).astype(BF16)


def _retproj(x, g, wq, wk, wv, wg, cos, sin):
    B, S, D = x.shape
    tm = ROW_TILE
    kw, vw = RET_HEADS * RET_KDIM, RET_HEADS * RET_VDIM
    table = pl.BlockSpec((tm, RET_KDIM // 2), lambda b, i: (i, 0))
    return pl.pallas_call(
        _retproj_kernel,
        grid=(B, S // tm),
        in_specs=[_rows(tm, D), _resident((1, D)), _resident(wq.shape), _resident(wk.shape),
                  _resident(wv.shape), _resident(wg.shape), table, table],
        out_specs=[_rows(tm, kw), _rows(tm, kw), _rows(tm, vw), _rows(tm, vw)],
        out_shape=[jax.ShapeDtypeStruct((B, S, kw), BF16), jax.ShapeDtypeStruct((B, S, kw), BF16),
                   jax.ShapeDtypeStruct((B, S, vw), BF16), jax.ShapeDtypeStruct((B, S, vw), BF16)],
        compiler_params=_params("parallel", "parallel"),
        name="ret_proj",
    )(x, g, wq, wk, wv, wg, cos, sin)


def _retention_kernel(q_ref, k_ref, v_ref, gate_ref, o_ref, state_ref, decay_ref):
    C = RET_CHUNK
    log_gs = [math.log1p(-(2.0 ** (-5.0 - h))) for h in range(RET_HEADS)]

    @pl.when((pl.program_id(0) == 0) & (pl.program_id(1) == 0))
    def _():
        ii = lax.broadcasted_iota(jnp.int32, (C, C), 0)
        jj = lax.broadcasted_iota(jnp.int32, (C, C), 1)
        diff = (ii - jj).astype(F32)
        for h in range(RET_HEADS):
            decay_ref[h] = jnp.where(diff >= 0, jnp.exp(log_gs[h] * jnp.maximum(diff, 0.0)), 0.0)

    @pl.when(pl.program_id(1) == 0)
    def _():
        state_ref[...] = jnp.zeros_like(state_ref)

    pos = lax.broadcasted_iota(jnp.int32, (C, 1), 0).astype(F32)

    for h in range(RET_HEADS):
        log_g = log_gs[h]
        inner_decay = decay_ref[h]
        q_decay = jnp.exp(log_g * (pos + 1.0))
        k_decay = jnp.exp(log_g * (C - 1.0 - pos))
        chunk_decay = math.exp(log_g * C)
        kcols = slice(h * RET_KDIM, (h + 1) * RET_KDIM)
        vcols = slice(h * RET_VDIM, (h + 1) * RET_VDIM)
        for c in range(q_ref.shape[0] // C):
            rows = slice(c * C, (c + 1) * C)
            qc = q_ref[rows, kcols]
            kc = k_ref[rows, kcols]
            vc = v_ref[rows, vcols]
            state = state_ref[h]
            scores = lax.dot_general(qc, kc, (((1,), (1,)), ((), ())),
                                     preferred_element_type=F32) * inner_decay
            o = (jnp.dot(scores.astype(BF16), vc, preferred_element_type=F32)
                 + jnp.dot(qc, state.astype(BF16), preferred_element_type=F32) * q_decay)
            kd = (kc.astype(F32) * k_decay).astype(BF16)
            state_ref[h] = state * chunk_decay + lax.dot_general(
                kd, vc, (((0,), (0,)), ((), ())), preferred_element_type=F32)
            mu = jnp.mean(o, axis=-1, keepdims=True)
            var = jnp.mean(jnp.square(o - mu), axis=-1, keepdims=True)
            on = (o - mu) * lax.rsqrt(var + EPS)
            o_ref[rows, vcols] = (gate_ref[rows, vcols].astype(F32) * on).astype(BF16)


def _retention(q, k, v, gate):
    B, S, kw = q.shape
    vw = v.shape[-1]
    tc = RET_TILE
    return pl.pallas_call(
        _retention_kernel,
        grid=(B, S // tc),
        in_specs=[_rows(tc, kw), _rows(tc, kw), _rows(tc, vw), _rows(tc, vw)],
        out_specs=_rows(tc, vw),
        out_shape=jax.ShapeDtypeStruct((B, S, vw), BF16),
        scratch_shapes=[pltpu.VMEM((RET_HEADS, RET_KDIM, RET_VDIM), F32),
                        pltpu.VMEM((RET_HEADS, RET_CHUNK, RET_CHUNK), F32)],
        compiler_params=_params("arbitrary", "arbitrary"),
        name="retention",
    )(q, k, v, gate)


def _proj_residual_kernel(x_ref, a_ref, w_ref, o_ref):
    o_ref[...] = x_ref[...] + jnp.dot(a_ref[...], w_ref[...], preferred_element_type=F32)


def _proj_residual(x, a, w):
    B, S, D = x.shape
    tm = ROW_TILE
    return pl.pallas_call(
        _proj_residual_kernel,
        grid=(B, S // tm),
        in_specs=[_rows(tm, D), _rows(tm, a.shape[-1]), _resident(w.shape)],
        out_specs=_rows(tm, D),
        out_shape=jax.ShapeDtypeStruct((B, S, D), F32),
        compiler_params=_params("parallel", "parallel"),
        name="ret_outproj",
    )(x, a, w)


def _head_sum_matrix():
    g = jnp.arange(A_WIDTH // 2) // A_HEAD_DIM
    return (g[:, None] == g[None, :]).astype(BF16)


def _head_expand_matrix():
    head_of_lane = jnp.arange(A_WIDTH) // A_HEAD_DIM
    return (jnp.arange(LANES)[:, None] == head_of_lane[None, :]).astype(BF16)


def kernel(x, even_norm, even_w_in, even_q_gain, even_k_gain, even_sconv_w, even_w_out, odd_norm,
           ret_wq, ret_wk, ret_wv, ret_wg, ret_gn_gain, ret_wo, ffn_norm, ffn_w_up, ffn_conv_w,
           ffn_w_down):
    B, S, D = x.shape
    depth = ffn_norm.shape[0]
    bf = lambda w: w.astype(BF16)
    row = lambda v: v.reshape(1, -1).astype(F32)
    hsum = _head_sum_matrix()
    expand = _head_expand_matrix()
    cos = sin = None

    for l in range(depth):
        if l % 2 == 0:
            e = l // 2
            qs, ks, vs, b = _inproj(x, row(even_norm[e]), bf(even_w_in[e]),
                                    row(jnp.tile(even_q_gain[e], A_HEADS) * _Q_SCALE),
                                    row(jnp.tile(even_k_gain[e], A_HEADS)),
                                    even_sconv_w[e], hsum)
            os, lses = zip(*[_dilated_attention_one(q, k, v, d)
                             for q, k, v, d in zip(qs, ks, vs, DILATIONS)])
            x = _outproj(x, os, lses, b, expand, bf(even_w_out[e]))
        else:
            o = l // 2
            if cos is None:
                cos, sin = _rope_table(S)
            q, k, v, gate = _retproj(x, row(odd_norm[o]), bf(ret_wq[o]), bf(ret_wk[o]),
                                     bf(ret_wv[o]), bf(ret_wg[o]), cos, sin)
            a = _retention(q, k, v, gate)
            x = _proj_residual(x, a, bf(ret_gn_gain[o].reshape(-1, 1) * ret_wo[o]))
        x = _ffn(x, row(ffn_norm[l]), bf(ffn_w_up[l]), ffn_conv_w[l], bf(ffn_w_down[l]))
    return x
```

```python
import functools
import math

import jax
import jax.numpy as jnp
from jax import lax
from jax.experimental import pallas as pl
from jax.experimental.pallas import tpu as pltpu

F32 = jnp.float32
BF16 = jnp.bfloat16

D_MODEL = 1024
A_HEADS = 8
A_HEAD_DIM = 64
A_WIDTH = A_HEADS * A_HEAD_DIM
B_WIDTH = D_MODEL - A_WIDTH
DILATIONS = (1, 4, 16)
REACH = 128
RET_HEADS = 4
RET_KDIM = D_MODEL // RET_HEADS
RET_VDIM = 2 * D_MODEL // RET_HEADS
RET_CHUNK = 256
ROPE_BASE = 10000.0
D_FF = 2816
EPS = 1e-6

SUBLANES = 8
LANES = 128
VMEM_LIMIT_BYTES = 56 * 1024 * 1024

ROW_TILE = 1024
FFN_ROW_TILE = 1024
FF_TILE = 256
ATTN_ROWS = 2048
RET_TILE = 1024


def _params(*semantics):
    return pltpu.CompilerParams(dimension_semantics=semantics, vmem_limit_bytes=VMEM_LIMIT_BYTES)


def _resident(shape):
    nd = len(shape)
    return pl.BlockSpec(shape, lambda *_: (0,) * nd, pipeline_mode=pl.Buffered(1))


def _rows(tm, width):
    return pl.BlockSpec((None, tm, width), lambda b, i: (b, i, 0))


def _stream_rows(dil, tm, width):
    return pl.BlockSpec((None, dil, tm // dil, width), lambda b, i: (b, 0, i, 0))


def _rms(x, g):
    ms = jnp.mean(x * x, axis=-1, keepdims=True)
    return x * lax.rsqrt(ms + EPS) * g


def _silu(g):
    return g / (1.0 + jnp.exp2(g * -math.log2(math.e)))


def _shift_rows(cur, prev, k):
    rolled = pltpu.roll(cur, k, axis=0)
    prev_rolled = pltpu.roll(prev, k, axis=0)
    row = lax.broadcasted_iota(jnp.int32, prev.shape, 0)
    head = jnp.where(row < k, prev_rolled, rolled[:SUBLANES])
    return jnp.concatenate([head, rolled[SUBLANES:]], axis=0)


def _causal_conv3(cur, prev, w):
    return (_shift_rows(cur, prev, 2) * w[0:1] + _shift_rows(cur, prev, 1) * w[1:2]
            + cur * w[2:3])


def _stream_group(dil):
    return max(LANES, 2 * SUBLANES * dil)


def _stream_perm_matrix(dil):
    group = _stream_group(dil)
    dst = jnp.arange(group)
    src = (dst % (group // dil)) * dil + dst // (group // dil)
    return (jnp.arange(group)[None, :] == src[:, None]).astype(BF16)


def _store_streams(t, perm_refs, out_refs):
    tm = t.shape[0]
    tb = t.astype(BF16)
    for dil, perm_ref, out_ref in zip(DILATIONS, perm_refs, out_refs):
        if dil == 1:
            out_ref[0] = tb
            continue
        group = perm_ref.shape[0]
        n = group // dil
        for g in range(tm // group):
            y = jnp.dot(perm_ref[...], tb[g * group:(g + 1) * group, :],
                        preferred_element_type=F32).astype(BF16)
            for r in range(dil):
                out_ref[r, g * n:(g + 1) * n, :] = y[r * n:(r + 1) * n, :]


def _inproj_kernel(x_ref, g_ref, w_ref, qg_ref, kg_ref, cw_ref, hsum_ref, *rest):
    nd = len(DILATIONS)
    perm_refs, rest = rest[:nd], rest[nd:]
    q_refs, k_refs, v_refs = rest[:nd], rest[nd:2 * nd], rest[2 * nd:3 * nd]
    b_ref, halo_ref = rest[3 * nd:]

    @pl.when(pl.program_id(1) == 0)
    def _():
        halo_ref[...] = jnp.zeros_like(halo_ref)

    xn = _rms(x_ref[...], g_ref[...]).astype(BF16)

    def proj(idx):
        return jnp.dot(xn, w_ref[:, idx * A_WIDTH:(idx + 1) * A_WIDTH], preferred_element_type=F32)

    def head_norm(t, gain):
        t2 = (t * t).astype(BF16)
        half = A_WIDTH // 2
        ss = jnp.concatenate(
            [jnp.dot(t2[:, :half], hsum_ref[...], preferred_element_type=F32),
             jnp.dot(t2[:, half:], hsum_ref[...], preferred_element_type=F32)], axis=1)
        return t * lax.rsqrt(ss * (1.0 / A_HEAD_DIM) + EPS) * gain

    _store_streams(head_norm(proj(0), qg_ref[...]), perm_refs, q_refs)
    _store_streams(head_norm(proj(1), kg_ref[...]), perm_refs, k_refs)
    _store_streams(proj(2), perm_refs, v_refs)
    gate_b = proj(3)
    s = proj(4) * proj(5)
    conv = _causal_conv3(s, halo_ref[...], cw_ref[...])
    halo_ref[...] = s[s.shape[0] - SUBLANES:]
    b_ref[...] = (gate_b * conv).astype(BF16)


def _inproj(x, g, w_in, q_gain, k_gain, sconv_w, hsum):
    B, S, D = x.shape
    tm = ROW_TILE
    streams = [jax.ShapeDtypeStruct((B, d, S // d, A_WIDTH), BF16) for d in DILATIONS]
    stream_specs = [_stream_rows(d, tm, A_WIDTH) for d in DILATIONS]
    perms = [_stream_perm_matrix(d) for d in DILATIONS]
    outs = pl.pallas_call(
        _inproj_kernel,
        grid=(B, S // tm),
        in_specs=[_rows(tm, D), _resident((1, D)), _resident(w_in.shape), _resident((1, A_WIDTH)),
                  _resident((1, A_WIDTH)), _resident(sconv_w.shape), _resident(hsum.shape)]
                 + [_resident(p.shape) for p in perms],
        out_specs=stream_specs * 3 + [_rows(tm, B_WIDTH)],
        out_shape=streams * 3 + [jax.ShapeDtypeStruct((B, S, B_WIDTH), BF16)],
        scratch_shapes=[pltpu.VMEM((SUBLANES, B_WIDTH), F32)],
        compiler_params=_params("parallel", "arbitrary"),
        name="even_inproj",
    )(x, g, w_in, q_gain, k_gain, sconv_w, hsum, *perms)
    nd = len(DILATIONS)
    return outs[:nd], outs[nd:2 * nd], outs[2 * nd:3 * nd], outs[3 * nd]


_Q_SCALE = A_HEAD_DIM ** -0.5 * math.log2(math.e)


def _col_reduce(x, op):
    parts = x.reshape(4, x.shape[0] // 4, x.shape[1])
    part = op(op(parts[0], parts[1]), op(parts[2], parts[3]))
    reduce = jnp.max if op is jnp.maximum else jnp.sum
    return reduce(part, axis=0, keepdims=True)


def _attn_kernel(q_ref, kp_ref, kc_ref, vp_ref, vc_ref, o_ref, lse_ref,
                 qt_scr, k_scr, vwin_scr, bias_scr, s_scr, ot_scr, lse_scr):
    first_tile = pl.program_id(2) == 0
    hd = A_HEAD_DIM
    n_pairs = A_WIDTH // LANES
    n_streams, tq = q_ref.shape[0], q_ref.shape[1]
    blocks_per_stream = tq // REACH
    n_blocks = n_streams * blocks_per_stream
    assert blocks_per_stream & (blocks_per_stream - 1) == 0 and n_blocks % 2 == 0

    kj = lax.broadcasted_iota(jnp.int32, (2 * REACH, 2 * REACH), 0)
    qi = lax.broadcasted_iota(jnp.int32, (2 * REACH, 2 * REACH), 1) % REACH
    band = (kj >= qi) & (kj <= qi + REACH)
    bias_scr[0] = jnp.where(band & ((kj >= REACH) | jnp.logical_not(first_tile)), 0.0, -1e30)
    bias_scr[1] = jnp.where(band, 0.0, -1e30)
    for s in range(n_streams):
        q_t = q_ref[s].T
        v_t = jnp.concatenate([vp_ref[s], vc_ref[s]], axis=0).T
        for j in range(blocks_per_stream):
            qt_scr[s * blocks_per_stream + j] = q_t[:, j * REACH:(j + 1) * REACH]
            vwin_scr[s * blocks_per_stream + j] = v_t[:, j * REACH:(j + 2) * REACH]
        for pr in range(n_pairs):
            slab = slice(pr * LANES, (pr + 1) * LANES)
            k_scr[pr, s, 0:REACH, :] = kp_ref[s, :, slab]
            k_scr[pr, s, REACH:, :] = kc_ref[s, :, slab]
    zeros = jnp.zeros((hd, REACH), BF16)
    ones_rows = jnp.ones((2 * SUBLANES, 2 * REACH), BF16)

    def block_scores(blk, slot):
        s = lax.shift_right_logical(blk, blocks_per_stream.bit_length() - 1)
        j = blk & (blocks_per_stream - 1)
        bias = bias_scr[jnp.minimum(j, 1)]
        maxima = []
        for pr in range(n_pairs):
            k_win = k_scr[pr, s, pl.ds(pl.multiple_of(j * REACH, REACH), 2 * REACH), :]
            qt = qt_scr[blk, pr * LANES:(pr + 1) * LANES, :]
            q_bd = jnp.concatenate([jnp.concatenate([qt[:hd], zeros], axis=1),
                                    jnp.concatenate([zeros, qt[hd:]], axis=1)], axis=0)
            sc = jnp.dot(k_win, q_bd, preferred_element_type=F32) + bias
            s_scr[slot, pr] = sc
            maxima.append(_col_reduce(sc, jnp.maximum))
        return tuple(maxima)

    def block_finish(blk, slot, maxima):
        for pr in range(n_pairs):
            p_t = jnp.exp2(s_scr[slot, pr] - maxima[pr]).astype(BF16)
            v_pair = vwin_scr[blk, pr * LANES:(pr + 1) * LANES, :]
            o_aug = jnp.dot(jnp.concatenate([v_pair, ones_rows], axis=0), p_t,
                            preferred_element_type=F32)
            den = o_aug[2 * hd:2 * hd + 1, :]
            o_t = o_aug[:2 * hd, :] / den
            ot_scr[blk, pr * LANES:(pr + 1) * LANES, :] = jnp.concatenate(
                [o_t[:hd, :REACH], o_t[hd:, REACH:]], axis=0).astype(BF16)
            lse = (maxima[pr] + jnp.log2(den)) * math.log(2.0)
            lse_scr[blk, pr] = jnp.broadcast_to(lse, (SUBLANES, 2 * REACH))

    def two_blocks(u, maxima_even, lookahead=True):
        blk = 2 * u
        maxima_odd = block_scores(blk + 1, 1)
        block_finish(blk, 0, maxima_even)
        maxima_next = block_scores(blk + 2, 0) if lookahead else None
        block_finish(blk + 1, 1, maxima_odd)
        return maxima_next

    maxima = lax.fori_loop(0, n_blocks // 2 - 1, two_blocks, block_scores(jnp.int32(0), 0))
    two_blocks(jnp.int32(n_blocks // 2 - 1), maxima, lookahead=False)

    for blk in range(n_blocks):
        s, j = divmod(blk, blocks_per_stream)
        rows = slice(j * REACH, (j + 1) * REACH)
        o_ref[s, rows, :] = ot_scr[blk].T
        lse_rows = []
        for pr in range(n_pairs):
            lse_rows += [lse_scr[blk, pr, 0:1, :REACH], lse_scr[blk, pr, 0:1, REACH:]]
        lse_rows.append(jnp.zeros((LANES - A_HEADS, REACH), F32))
        lse_ref[s, rows, :] = jnp.concatenate(lse_rows, axis=0).T


def _dilated_attention_one(q, k, v, dil):
    B, _, L, W = q.shape
    tq = min(ATTN_ROWS, L)
    ns = min(ATTN_ROWS // tq, dil)
    n_blocks, n_pairs = ns * tq // REACH, W // LANES
    cur = pl.BlockSpec((None, ns, tq, W), lambda b, r, i: (b, r, i, 0))
    prev = pl.BlockSpec((None, ns, REACH, W),
                        lambda b, r, i: (b, r, jnp.maximum(i * (tq // REACH) - 1, 0), 0))
    return pl.pallas_call(
        _attn_kernel,
        grid=(B, dil // ns, L // tq),
        in_specs=[cur, prev, cur, prev, cur],
        out_specs=[cur, pl.BlockSpec((None, ns, tq, LANES), lambda b, r, i: (b, r, i, 0))],
        out_shape=[jax.ShapeDtypeStruct((B, dil, L, W), BF16),
                   jax.ShapeDtypeStruct((B, dil, L, LANES), F32)],
        scratch_shapes=[
            pltpu.VMEM((n_blocks, W, REACH), BF16),
            pltpu.VMEM((n_pairs, ns, REACH + tq, LANES), BF16),
            pltpu.VMEM((n_blocks, W, 2 * REACH), BF16),
            pltpu.VMEM((2, 2 * REACH, 2 * REACH), F32),
            pltpu.VMEM((2, n_pairs, 2 * REACH, 2 * REACH), F32),
            pltpu.VMEM((n_blocks, W, REACH), BF16),
            pltpu.VMEM((n_blocks, n_pairs, SUBLANES, 2 * REACH), F32),
        ],
        compiler_params=_params("parallel", "parallel", "arbitrary"),
        name=f"dilated_attn_d{dil}",
    )(q, k, k, v, v)


def _load_streams(src_ref, dil, slab_ref):
    n_slabs = src_ref.shape[-1] // LANES
    if dil == 1:
        t = src_ref[0].astype(F32)
        return [t[:, s * LANES:(s + 1) * LANES] for s in range(n_slabs)]
    n = src_ref.shape[1]
    for r in range(dil):
        for s in range(n_slabs):
            slab_ref[s, pl.ds(r, n, stride=dil), :] = (
                src_ref[r, :, s * LANES:(s + 1) * LANES].astype(F32))
    return [slab_ref[s] for s in range(n_slabs)]


def _outproj_kernel(x_ref, *rest):
    nd = len(DILATIONS)
    o_refs, l_refs = rest[:nd], rest[nd:2 * nd]
    b_ref, expand_ref, w_ref, out_ref, o_slab_ref, l_slab_ref = rest[2 * nd:]
    n_slabs = A_WIDTH // LANES

    ls = [_load_streams(l_ref, d, l_slab_ref.at[n])[0]
          for n, (d, l_ref) in enumerate(zip(DILATIONS, l_refs))]
    mx = functools.reduce(jnp.maximum, ls)
    es = [jnp.exp(l - mx) for l in ls]
    tot = functools.reduce(jnp.add, es)
    a_slabs = [None] * n_slabs
    for n, (d, e, o_ref) in enumerate(zip(DILATIONS, es, o_refs)):
        w = e / tot
        hi = w.astype(BF16)
        lo = (w - hi.astype(F32)).astype(BF16)
        wf = (jnp.dot(hi, expand_ref[...], preferred_element_type=F32)
              + jnp.dot(lo, expand_ref[...], preferred_element_type=F32))
        o_slabs = _load_streams(o_ref, d, o_slab_ref.at[n])
        for s in range(n_slabs):
            term = wf[:, s * LANES:(s + 1) * LANES] * o_slabs[s]
            a_slabs[s] = term if a_slabs[s] is None else a_slabs[s] + term
    a = jnp.concatenate(a_slabs, axis=1).astype(BF16)
    y = (jnp.dot(a, w_ref[:A_WIDTH, :], preferred_element_type=F32)
         + jnp.dot(b_ref[...], w_ref[A_WIDTH:, :], preferred_element_type=F32))
    out_ref[...] = x_ref[...] + y


def _outproj(x, os, lses, b, expand, w_out):
    B, S, D = x.shape
    tm = ROW_TILE
    nd = len(DILATIONS)
    return pl.pallas_call(
        _outproj_kernel,
        grid=(B, S // tm),
        in_specs=[_rows(tm, D)] + [_stream_rows(d, tm, A_WIDTH) for d in DILATIONS]
                 + [_stream_rows(d, tm, LANES) for d in DILATIONS]
                 + [_rows(tm, B_WIDTH), _resident(expand.shape), _resident(w_out.shape)],
        out_specs=_rows(tm, D),
        out_shape=jax.ShapeDtypeStruct((B, S, D), F32),
        scratch_shapes=[pltpu.VMEM((nd, A_WIDTH // LANES, tm, LANES), F32),
                        pltpu.VMEM((nd, 1, tm, LANES), F32)],
        compiler_params=_params("parallel", "parallel"),
        name="even_outproj",
    )(x, *os, *lses, b, expand, w_out)


def _ffn_kernel(x_ref, g_ref, wup_ref, cw_ref, wdn_ref, o_ref, h_ref, halo_ref):
    @pl.when(pl.program_id(1) == 0)
    def _():
        halo_ref[...] = jnp.zeros_like(halo_ref)

    x = x_ref[...]
    tm = x.shape[0]
    xn = _rms(x, g_ref[...]).astype(BF16)

    def conv_up(col):
        cols = slice(col, col + FF_TILE)
        u = jnp.dot(xn, wup_ref[:, cols], preferred_element_type=F32)
        y = _causal_conv3(u, halo_ref[:, cols], cw_ref[:, cols])
        halo_ref[:, cols] = u[tm - SUBLANES:]
        return y

    for c in range(D_FF // FF_TILE):
        gate = conv_up(c * FF_TILE)
        val = conv_up(D_FF + c * FF_TILE)
        h_ref[:, c * FF_TILE:(c + 1) * FF_TILE] = (_silu(gate) * val).astype(BF16)
    o_ref[...] = x + jnp.dot(h_ref[...], wdn_ref[...], preferred_element_type=F32)


def _ffn(x, g, w_up, conv_w, w_down):
    B, S, D = x.shape
    tm = FFN_ROW_TILE
    return pl.pallas_call(
        _ffn_kernel,
        grid=(B, S // tm),
        in_specs=[_rows(tm, D), _resident((1, D)), _resident(w_up.shape), _resident(conv_w.shape),
                  _resident(w_down.shape)],
        out_specs=_rows(tm, D),
        out_shape=jax.ShapeDtypeStruct((B, S, D), F32),
        scratch_shapes=[pltpu.VMEM((tm, D_FF), BF16), pltpu.VMEM((SUBLANES, 2 * D_FF), F32)],
        compiler_params=_params("parallel", "arbitrary"),
        name="conv_glu_ffn",
    )(x, g, w_up, conv_w, w_down)


def _rope_table_kernel(cos_ref, sin_ref):
    half = cos_ref.shape[1]
    pos = lax.broadcasted_iota(jnp.int32, cos_ref.shape, 0).astype(F32)
    j = lax.broadcasted_iota(jnp.int32, cos_ref.shape, 1).astype(F32)
    inv = jnp.exp(j * (-math.log(ROPE_BASE) / half))
    ang = pos * inv
    cos_ref[...] = jnp.cos(ang)
    sin_ref[...] = jnp.sin(ang)


def _rope_table(S):
    half = RET_KDIM // 2
    shape = jax.ShapeDtypeStruct((S, half), F32)
    return pl.pallas_call(_rope_table_kernel, out_shape=[shape, shape], name="rope_table")()


def _retproj_kernel(x_ref, g_ref, wq_ref, wk_ref, wv_ref, wg_ref, cos_ref, sin_ref,
                    q_ref, k_ref, v_ref, gate_ref):
    xn = _rms(x_ref[...], g_ref[...]).astype(BF16)
    cos = cos_ref[...]
    sin = sin_ref[...]
    half = RET_KDIM // 2

    def rotary_store(t, dst_ref):
        for h in range(RET_HEADS):
            x1 = t[:, h * RET_KDIM:h * RET_KDIM + half]
            x2 = t[:, h * RET_KDIM + half:(h + 1) * RET_KDIM]
            dst_ref[:, h * RET_KDIM:h * RET_KDIM + half] = (x1 * cos - x2 * sin).astype(BF16)
            dst_ref[:, h * RET_KDIM + half:(h + 1) * RET_KDIM] = (x1 * sin + x2 * cos).astype(BF16)

    rotary_store(jnp.dot(xn, wq_ref[...], preferred_element_type=F32), q_ref)
    rotary_store(jnp.dot(xn, wk_ref[...], preferred_element_type=F32) * (RET_KDIM ** -0.5), k_ref)
    v_ref[...] = jnp.dot(xn, wv_ref[...], preferred_element_type=F32).astype(BF16)
    gate_ref[...] = _silu(jnp.dot(xn, wg_ref[...], preferred_element_type=F32)).astype(BF16)


def _retproj(x, g, wq, wk, wv, wg, cos, sin):
    B, S, D = x.shape
    tm = ROW_TILE
    kw, vw = RET_HEADS * RET_KDIM, RET_HEADS * RET_VDIM
    table = pl.BlockSpec((tm, RET_KDIM // 2), lambda b, i: (i, 0))
    return pl.pallas_call(
        _retproj_kernel,
        grid=(B, S // tm),
        in_specs=[_rows(tm, D), _resident((1, D)), _resident(wq.shape), _resident(wk.shape),
                  _resident(wv.shape), _resident(wg.shape), table, table],
        out_specs=[_rows(tm, kw), _rows(tm, kw), _rows(tm, vw), _rows(tm, vw)],
        out_shape=[jax.ShapeDtypeStruct((B, S, kw), BF16), jax.ShapeDtypeStruct((B, S, kw), BF16),
                   jax.ShapeDtypeStruct((B, S, vw), BF16), jax.ShapeDtypeStruct((B, S, vw), BF16)],
        compiler_params=_params("parallel", "parallel"),
        name="ret_proj",
    )(x, g, wq, wk, wv, wg, cos, sin)


def _retention_kernel(q_ref, k_ref, v_ref, gate_ref, o_ref, state_ref, decay_ref):
    C = RET_CHUNK
    log_gs = [math.log1p(-(2.0 ** (-5.0 - h))) for h in range(RET_HEADS)]

    @pl.when((pl.program_id(0) == 0) & (pl.program_id(1) == 0))
    def _():
        ii = lax.broadcasted_iota(jnp.int32, (C, C), 0)
        jj = lax.broadcasted_iota(jnp.int32, (C, C), 1)
        diff = (ii - jj).astype(F32)
        for h in range(RET_HEADS):
            decay_ref[h] = jnp.where(diff >= 0, jnp.exp(log_gs[h] * jnp.maximum(diff, 0.0)), 0.0)

    @pl.when(pl.program_id(1) == 0)
    def _():
        state_ref[...] = jnp.zeros_like(state_ref)

    pos = lax.broadcasted_iota(jnp.int32, (C, 1), 0).astype(F32)

    for h in range(RET_HEADS):
        log_g = log_gs[h]
        inner_decay = decay_ref[h]
        q_decay = jnp.exp(log_g * (pos + 1.0))
        k_decay = jnp.exp(log_g * (C - 1.0 - pos))
        chunk_decay = math.exp(log_g * C)
        kcols = slice(h * RET_KDIM, (h + 1) * RET_KDIM)
        vcols = slice(h * RET_VDIM, (h + 1) * RET_VDIM)
        for c in range(q_ref.shape[0] // C):
            rows = slice(c * C, (c + 1) * C)
            qc = q_ref[rows, kcols]
            kc = k_ref[rows, kcols]
            vc = v_ref[rows, vcols]
            state = state_ref[h]
            scores = lax.dot_general(qc, kc, (((1,), (1,)), ((), ())),
                                     preferred_element_type=F32) * inner_decay
            o = (jnp.dot(scores.astype(BF16), vc, preferred_element_type=F32)
                 + jnp.dot(qc, state.astype(BF16), preferred_element_type=F32) * q_decay)
            kd = (kc.astype(F32) * k_decay).astype(BF16)
            state_ref[h] = state * chunk_decay + lax.dot_general(
                kd, vc, (((0,), (0,)), ((), ())), preferred_element_type=F32)
            mu = jnp.mean(o, axis=-1, keepdims=True)
            var = jnp.mean(jnp.square(o - mu), axis=-1, keepdims=True)
            on = (o - mu) * lax.rsqrt(var + EPS)
            o_ref[rows, vcols] = (gate_ref[rows, vcols].astype(F32) * on).astype(BF16)


def _retention(q, k, v, gate):
    B, S, kw = q.shape
    vw = v.shape[-1]
    tc = RET_TILE
    return pl.pallas_call(
        _retention_kernel,
        grid=(B, S // tc),
        in_specs=[_rows(tc, kw), _rows(tc, kw), _rows(tc, vw), _rows(tc, vw)],
        out_specs=_rows(tc, vw),
        out_shape=jax.ShapeDtypeStruct((B, S, vw), BF16),
        scratch_shapes=[pltpu.VMEM((RET_HEADS, RET_KDIM, RET_VDIM), F32),
                        pltpu.VMEM((RET_HEADS, RET_CHUNK, RET_CHUNK), F32)],
        compiler_params=_params("arbitrary", "arbitrary"),
        name="retention",
    )(q, k, v, gate)


def _proj_residual_kernel(x_ref, a_ref, w_ref, o_ref):
    o_ref[...] = x_ref[...] + jnp.dot(a_ref[...], w_ref[...], preferred_element_type=F32)


def _proj_residual(x, a, w):
    B, S, D = x.shape
    tm = ROW_TILE
    return pl.pallas_call(
        _proj_residual_kernel,
        grid=(B, S // tm),
        in_specs=[_rows(tm, D), _rows(tm, a.shape[-1]), _resident(w.shape)],
        out_specs=_rows(tm, D),
        out_shape=jax.ShapeDtypeStruct((B, S, D), F32),
        compiler_params=_params("parallel", "parallel"),
        name="ret_outproj",
    )(x, a, w)


def _head_sum_matrix():
    g = jnp.arange(A_WIDTH // 2) // A_HEAD_DIM
    return (g[:, None] == g[None, :]).astype(BF16)


def _head_expand_matrix():
    head_of_lane = jnp.arange(A_WIDTH) // A_HEAD_DIM
    return (jnp.arange(LANES)[:, None] == head_of_lane[None, :]).astype(BF16)


def kernel(x, even_norm, even_w_in, even_q_gain, even_k_gain, even_sconv_w, even_w_out, odd_norm,
           ret_wq, ret_wk, ret_wv, ret_wg, ret_gn_gain, ret_wo, ffn_norm, ffn_w_up, ffn_conv_w,
           ffn_w_down):
    B, S, D = x.shape
    depth = ffn_norm.shape[0]
    bf = lambda w: w.astype(BF16)
    row = lambda v: v.reshape(1, -1).astype(F32)
    hsum = _head_sum_matrix()
    expand = _head_expand_matrix()
    cos = sin = None

    for l in range(depth):
        if l % 2 == 0:
            e = l // 2
            qs, ks, vs, b = _inproj(x, row(even_norm[e]), bf(even_w_in[e]),
                                    row(jnp.tile(even_q_gain[e], A_HEADS) * _Q_SCALE),
                                    row(jnp.tile(even_k_gain[e], A_HEADS)),
                                    even_sconv_w[e], hsum)
            os, lses = zip(*[_dilated_attention_one(q, k, v, d)
                             for q, k, v, d in zip(qs, ks, vs, DILATIONS)])
            x = _outproj(x, os, lses, b, expand, bf(even_w_out[e]))
        else:
            o = l // 2
            if cos is None:
                cos, sin = _rope_table(S)
            q, k, v, gate = _retproj(x, row(odd_norm[o]), bf(ret_wq[o]), bf(ret_wk[o]),
                                     bf(ret_wv[o]), bf(ret_wg[o]), cos, sin)
            a = _retention(q, k, v, gate)
            x = _proj_residual(x, a, bf(ret_gn_gain[o].reshape(-1, 1) * ret_wo[o]))
        x = _ffn(x, row(ffn_norm[l]), bf(ffn_w_up[l]), ffn_conv_w[l], bf(ffn_w_down[l]))
    return x
```

```python
import functools
import math

import jax
import jax.numpy as jnp
from jax import lax
from jax.experimental import pallas as pl
from jax.experimental.pallas import tpu as pltpu

F32 = jnp.float32
BF16 = jnp.bfloat16

D_MODEL = 1024
A_HEADS = 8
A_HEAD_DIM = 64
A_WIDTH = A_HEADS * A_HEAD_DIM
B_WIDTH = D_MODEL - A_WIDTH
DILATIONS = (1, 4, 16)
REACH = 128
RET_HEADS = 4
RET_KDIM = D_MODEL // RET_HEADS
RET_VDIM = 2 * D_MODEL // RET_HEADS
RET_CHUNK = 256
ROPE_BASE = 10000.0
D_FF = 2816
EPS = 1e-6

SUBLANES = 8
LANES = 128
VMEM_LIMIT_BYTES = 56 * 1024 * 1024

ROW_TILE = 1024
FFN_ROW_TILE = 1024
FF_TILE = 256
ATTN_ROWS = 2048
RET_TILE = 512


def _params(*semantics):
    return pltpu.CompilerParams(dimension_semantics=semantics, vmem_limit_bytes=VMEM_LIMIT_BYTES)


def _resident(shape):
    nd = len(shape)
    return pl.BlockSpec(shape, lambda *_: (0,) * nd, pipeline_mode=pl.Buffered(1))


def _rows(tm, width):
    return pl.BlockSpec((None, tm, width), lambda b, i: (b, i, 0))


def _stream_rows(dil, tm, width):
    return pl.BlockSpec((None, dil, tm // dil, width), lambda b, i: (b, 0, i, 0))


def _rms(x, g):
    ms = jnp.mean(x * x, axis=-1, keepdims=True)
    return x * lax.rsqrt(ms + EPS) * g


def _silu(g):
    return g / (1.0 + jnp.exp2(g * -math.log2(math.e)))


def _shift_rows(cur, prev, k):
    rolled = pltpu.roll(cur, k, axis=0)
    prev_rolled = pltpu.roll(prev, k, axis=0)
    row = lax.broadcasted_iota(jnp.int32, prev.shape, 0)
    head = jnp.where(row < k, prev_rolled, rolled[:SUBLANES])
    return jnp.concatenate([head, rolled[SUBLANES:]], axis=0)


def _causal_conv3(cur, prev, w):
    return (_shift_rows(cur, prev, 2) * w[0:1] + _shift_rows(cur, prev, 1) * w[1:2]
            + cur * w[2:3])


def _stream_group(dil):
    return max(LANES, 2 * SUBLANES * dil)


def _stream_perm_matrix(dil):
    group = _stream_group(dil)
    dst = jnp.arange(group)
    src = (dst % (group // dil)) * dil + dst // (group // dil)
    return (jnp.arange(group)[None, :] == src[:, None]).astype(BF16)


def _store_streams(t, perm_refs, out_refs):
    tm = t.shape[0]
    tb = t.astype(BF16)
    for dil, perm_ref, out_ref in zip(DILATIONS, perm_refs, out_refs):
        if dil == 1:
            out_ref[0] = tb
            continue
        group = perm_ref.shape[0]
        n = group // dil
        for g in range(tm // group):
            y = jnp.dot(perm_ref[...], tb[g * group:(g + 1) * group, :],
                        preferred_element_type=F32).astype(BF16)
            for r in range(dil):
                out_ref[r, g * n:(g + 1) * n, :] = y[r * n:(r + 1) * n, :]


def _inproj_kernel(x_ref, g_ref, w_ref, qg_ref, kg_ref, cw_ref, hsum_ref, *rest):
    nd = len(DILATIONS)
    perm_refs, rest = rest[:nd], rest[nd:]
    q_refs, k_refs, v_refs = rest[:nd], rest[nd:2 * nd], rest[2 * nd:3 * nd]
    b_ref, halo_ref = rest[3 * nd:]

    @pl.when(pl.program_id(1) == 0)
    def _():
        halo_ref[...] = jnp.zeros_like(halo_ref)

    xn = _rms(x_ref[...], g_ref[...]).astype(BF16)

    def proj(idx):
        return jnp.dot(xn, w_ref[:, idx * A_WIDTH:(idx + 1) * A_WIDTH], preferred_element_type=F32)

    def head_norm(t, gain):
        t2 = (t * t).astype(BF16)
        half = A_WIDTH // 2
        ss = jnp.concatenate(
            [jnp.dot(t2[:, :half], hsum_ref[...], preferred_element_type=F32),
             jnp.dot(t2[:, half:], hsum_ref[...], preferred_element_type=F32)], axis=1)
        return t * lax.rsqrt(ss * (1.0 / A_HEAD_DIM) + EPS) * gain

    _store_streams(head_norm(proj(0), qg_ref[...]), perm_refs, q_refs)
    _store_streams(head_norm(proj(1), kg_ref[...]), perm_refs, k_refs)
    _store_streams(proj(2), perm_refs, v_refs)
    gate_b = proj(3)
    s = proj(4) * proj(5)
    conv = _causal_conv3(s, halo_ref[...], cw_ref[...])
    halo_ref[...] = s[s.shape[0] - SUBLANES:]
    b_ref[...] = (gate_b * conv).astype(BF16)


def _inproj(x, g, w_in, q_gain, k_gain, sconv_w, hsum):
    B, S, D = x.shape
    tm = ROW_TILE
    streams = [jax.ShapeDtypeStruct((B, d, S // d, A_WIDTH), BF16) for d in DILATIONS]
    stream_specs = [_stream_rows(d, tm, A_WIDTH) for d in DILATIONS]
    perms = [_stream_perm_matrix(d) for d in DILATIONS]
    outs = pl.pallas_call(
        _inproj_kernel,
        grid=(B, S // tm),
        in_specs=[_rows(tm, D), _resident((1, D)), _resident(w_in.shape), _resident((1, A_WIDTH)),
                  _resident((1, A_WIDTH)), _resident(sconv_w.shape), _resident(hsum.shape)]
                 + [_resident(p.shape) for p in perms],
        out_specs=stream_specs * 3 + [_rows(tm, B_WIDTH)],
        out_shape=streams * 3 + [jax.ShapeDtypeStruct((B, S, B_WIDTH), BF16)],
        scratch_shapes=[pltpu.VMEM((SUBLANES, B_WIDTH), F32)],
        compiler_params=_params("parallel", "arbitrary"),
        name="even_inproj",
    )(x, g, w_in, q_gain, k_gain, sconv_w, hsum, *perms)
    nd = len(DILATIONS)
    return outs[:nd], outs[nd:2 * nd], outs[2 * nd:3 * nd], outs[3 * nd]


_Q_SCALE = A_HEAD_DIM ** -0.5 * math.log2(math.e)


def _col_reduce(x, op):
    parts = x.reshape(4, x.shape[0] // 4, x.shape[1])
    part = op(op(parts[0], parts[1]), op(parts[2], parts[3]))
    reduce = jnp.max if op is jnp.maximum else jnp.sum
    return reduce(part, axis=0, keepdims=True)


def _attn_kernel(q_ref, kp_ref, kc_ref, vp_ref, vc_ref, o_ref, lse_ref,
                 qt_scr, k_scr, vwin_scr, bias_scr, s_scr, ot_scr, lse_scr):
    first_tile = pl.program_id(2) == 0
    hd = A_HEAD_DIM
    n_pairs = A_WIDTH // LANES
    n_streams, tq = q_ref.shape[0], q_ref.shape[1]
    blocks_per_stream = tq // REACH
    n_blocks = n_streams * blocks_per_stream
    assert blocks_per_stream & (blocks_per_stream - 1) == 0 and n_blocks % 2 == 0

    kj = lax.broadcasted_iota(jnp.int32, (2 * REACH, 2 * REACH), 0)
    qi = lax.broadcasted_iota(jnp.int32, (2 * REACH, 2 * REACH), 1) % REACH
    band = (kj >= qi) & (kj <= qi + REACH)
    bias_scr[0] = jnp.where(band & ((kj >= REACH) | jnp.logical_not(first_tile)), 0.0, -1e30)
    bias_scr[1] = jnp.where(band, 0.0, -1e30)
    for s in range(n_streams):
        q_t = q_ref[s].T
        v_t = jnp.concatenate([vp_ref[s], vc_ref[s]], axis=0).T
        for j in range(blocks_per_stream):
            qt_scr[s * blocks_per_stream + j] = q_t[:, j * REACH:(j + 1) * REACH]
            vwin_scr[s * blocks_per_stream + j] = v_t[:, j * REACH:(j + 2) * REACH]
        for pr in range(n_pairs):
            slab = slice(pr * LANES, (pr + 1) * LANES)
            k_scr[pr, s, 0:REACH, :] = kp_ref[s, :, slab]
            k_scr[pr, s, REACH:, :] = kc_ref[s, :, slab]
    zeros = jnp.zeros((hd, REACH), BF16)
    ones_rows = jnp.ones((2 * SUBLANES, 2 * REACH), BF16)

    def block_scores(blk, slot):
        s = lax.shift_right_logical(blk, blocks_per_stream.bit_length() - 1)
        j = blk & (blocks_per_stream - 1)
        bias = bias_scr[jnp.minimum(j, 1)]
        maxima = []
        for pr in range(n_pairs):
            k_win = k_scr[pr, s, pl.ds(pl.multiple_of(j * REACH, REACH), 2 * REACH), :]
            qt = qt_scr[blk, pr * LANES:(pr + 1) * LANES, :]
            q_bd = jnp.concatenate([jnp.concatenate([qt[:hd], zeros], axis=1),
                                    jnp.concatenate([zeros, qt[hd:]], axis=1)], axis=0)
            sc = jnp.dot(k_win, q_bd, preferred_element_type=F32) + bias
            s_scr[slot, pr] = sc
            maxima.append(_col_reduce(sc, jnp.maximum))
        return tuple(maxima)

    def block_finish(blk, slot, maxima):
        for pr in range(n_pairs):
            p_t = jnp.exp2(s_scr[slot, pr] - maxima[pr]).astype(BF16)
            v_pair = vwin_scr[blk, pr * LANES:(pr + 1) * LANES, :]
            o_aug = jnp.dot(jnp.concatenate([v_pair, ones_rows], axis=0), p_t,
                            preferred_element_type=F32)
            den = o_aug[2 * hd:2 * hd + 1, :]
            o_t = o_aug[:2 * hd, :] / den
            ot_scr[blk, pr * LANES:(pr + 1) * LANES, :] = jnp.concatenate(
                [o_t[:hd, :REACH], o_t[hd:, REACH:]], axis=0).astype(BF16)
            lse = (maxima[pr] + jnp.log2(den)) * math.log(2.0)
            lse_scr[blk, pr] = jnp.broadcast_to(lse, (SUBLANES, 2 * REACH))

    def two_blocks(u, maxima_even, lookahead=True):
        blk = 2 * u
        maxima_odd = block_scores(blk + 1, 1)
        block_finish(blk, 0, maxima_even)
        maxima_next = block_scores(blk + 2, 0) if lookahead else None
        block_finish(blk + 1, 1, maxima_odd)
        return maxima_next

    maxima = lax.fori_loop(0, n_blocks // 2 - 1, two_blocks, block_scores(jnp.int32(0), 0))
    two_blocks(jnp.int32(n_blocks // 2 - 1), maxima, lookahead=False)

    for blk in range(n_blocks):
        s, j = divmod(blk, blocks_per_stream)
        rows = slice(j * REACH, (j + 1) * REACH)
        o_ref[s, rows, :] = ot_scr[blk].T
        lse_rows = []
        for pr in range(n_pairs):
            lse_rows += [lse_scr[blk, pr, 0:1, :REACH], lse_scr[blk, pr, 0:1, REACH:]]
        lse_rows.append(jnp.zeros((LANES - A_HEADS, REACH), F32))
        lse_ref[s, rows, :] = jnp.concatenate(lse_rows, axis=0).T


def _dilated_attention_one(q, k, v, dil):
    B, _, L, W = q.shape
    tq = min(ATTN_ROWS, L)
    ns = min(ATTN_ROWS // tq, dil)
    n_blocks, n_pairs = ns * tq // REACH, W // LANES
    cur = pl.BlockSpec((None, ns, tq, W), lambda b, r, i: (b, r, i, 0))
    prev = pl.BlockSpec((None, ns, REACH, W),
                        lambda b, r, i: (b, r, jnp.maximum(i * (tq // REACH) - 1, 0), 0))
    return pl.pallas_call(
        _attn_kernel,
        grid=(B, dil // ns, L // tq),
        in_specs=[cur, prev, cur, prev, cur],
        out_specs=[cur, pl.BlockSpec((None, ns, tq, LANES), lambda b, r, i: (b, r, i, 0))],
        out_shape=[jax.ShapeDtypeStruct((B, dil, L, W), BF16),
                   jax.ShapeDtypeStruct((B, dil, L, LANES), F32)],
        scratch_shapes=[
            pltpu.VMEM((n_blocks, W, REACH), BF16),
            pltpu.VMEM((n_pairs, ns, REACH + tq, LANES), BF16),
            pltpu.VMEM((n_blocks, W, 2 * REACH), BF16),
            pltpu.VMEM((2, 2 * REACH, 2 * REACH), F32),
            pltpu.VMEM((2, n_pairs, 2 * REACH, 2 * REACH), F32),
            pltpu.VMEM((n_blocks, W, REACH), BF16),
            pltpu.VMEM((n_blocks, n_pairs, SUBLANES, 2 * REACH), F32),
        ],
        compiler_params=_params("parallel", "parallel", "arbitrary"),
        name=f"dilated_attn_d{dil}",
    )(q, k, k, v, v)


def _load_streams(src_ref, dil, slab_ref):
    n_slabs = src_ref.shape[-1] // LANES
    if dil == 1:
        t = src_ref[0].astype(F32)
        return [t[:, s * LANES:(s + 1) * LANES] for s in range(n_slabs)]
    n = src_ref.shape[1]
    for r in range(dil):
        for s in range(n_slabs):
            slab_ref[s, pl.ds(r, n, stride=dil), :] = (
                src_ref[r, :, s * LANES:(s + 1) * LANES].astype(F32))
    return [slab_ref[s] for s in range(n_slabs)]


def _outproj_kernel(x_ref, *rest):
    nd = len(DILATIONS)
    o_refs, l_refs = rest[:nd], rest[nd:2 * nd]
    b_ref, expand_ref, w_ref, out_ref, o_slab_ref, l_slab_ref = rest[2 * nd:]
    n_slabs = A_WIDTH // LANES

    ls = [_load_streams(l_ref, d, l_slab_ref.at[n])[0]
          for n, (d, l_ref) in enumerate(zip(DILATIONS, l_refs))]
    mx = functools.reduce(jnp.maximum, ls)
    es = [jnp.exp(l - mx) for l in ls]
    tot = functools.reduce(jnp.add, es)
    a_slabs = [None] * n_slabs
    for n, (d, e, o_ref) in enumerate(zip(DILATIONS, es, o_refs)):
        w = e / tot
        wf = jnp.dot(w.astype(BF16), expand_ref[...], preferred_element_type=F32)
        o_slabs = _load_streams(o_ref, d, o_slab_ref.at[n])
        for s in range(n_slabs):
            term = wf[:, s * LANES:(s + 1) * LANES] * o_slabs[s]
            a_slabs[s] = term if a_slabs[s] is None else a_slabs[s] + term
    a = jnp.concatenate(a_slabs, axis=1).astype(BF16)
    y = (jnp.dot(a, w_ref[:A_WIDTH, :], preferred_element_type=F32)
         + jnp.dot(b_ref[...], w_ref[A_WIDTH:, :], preferred_element_type=F32))
    out_ref[...] = x_ref[...] + y


def _outproj(x, os, lses, b, expand, w_out):
    B, S, D = x.shape
    tm = ROW_TILE
    nd = len(DILATIONS)
    return pl.pallas_call(
        _outproj_kernel,
        grid=(B, S // tm),
        in_specs=[_rows(tm, D)] + [_stream_rows(d, tm, A_WIDTH) for d in DILATIONS]
                 + [_stream_rows(d, tm, LANES) for d in DILATIONS]
                 + [_rows(tm, B_WIDTH), _resident(expand.shape), _resident(w_out.shape)],
        out_specs=_rows(tm, D),
        out_shape=jax.ShapeDtypeStruct((B, S, D), F32),
        scratch_shapes=[pltpu.VMEM((nd, A_WIDTH // LANES, tm, LANES), F32),
                        pltpu.VMEM((nd, 1, tm, LANES), F32)],
        compiler_params=_params("parallel", "parallel"),
        name="even_outproj",
    )(x, *os, *lses, b, expand, w_out)


def _ffn_kernel(x_ref, g_ref, wup_ref, cw_ref, wdn_ref, o_ref, h_ref, halo_ref):
    @pl.when(pl.program_id(1) == 0)
    def _():
        halo_ref[...] = jnp.zeros_like(halo_ref)

    x = x_ref[...]
    tm = x.shape[0]
    xn = _rms(x, g_ref[...]).astype(BF16)

    def conv_up(col):
        cols = slice(col, col + FF_TILE)
        u = jnp.dot(xn, wup_ref[:, cols], preferred_element_type=F32)
        y = _causal_conv3(u, halo_ref[:, cols], cw_ref[:, cols])
        halo_ref[:, cols] = u[tm - SUBLANES:]
        return y

    for c in range(D_FF // FF_TILE):
        gate = conv_up(c * FF_TILE)
        val = conv_up(D_FF + c * FF_TILE)
        h_ref[:, c * FF_TILE:(c + 1) * FF_TILE] = (_silu(gate) * val).astype(BF16)
    o_ref[...] = x + jnp.dot(h_ref[...], wdn_ref[...], preferred_element_type=F32)


def _ffn(x, g, w_up, conv_w, w_down):
    B, S, D = x.shape
    tm = FFN_ROW_TILE
    return pl.pallas_call(
        _ffn_kernel,
        grid=(B, S // tm),
        in_specs=[_rows(tm, D), _resident((1, D)), _resident(w_up.shape), _resident(conv_w.shape),
                  _resident(w_down.shape)],
        out_specs=_rows(tm, D),
        out_shape=jax.ShapeDtypeStruct((B, S, D), F32),
        scratch_shapes=[pltpu.VMEM((tm, D_FF), BF16), pltpu.VMEM((SUBLANES, 2 * D_FF), F32)],
        compiler_params=_params("parallel", "arbitrary"),
        name="conv_glu_ffn",
    )(x, g, w_up, conv_w, w_down)


def _rope_table_kernel(cos_ref, sin_ref):
    half = cos_ref.shape[1]
    pos = lax.broadcasted_iota(jnp.int32, cos_ref.shape, 0).astype(F32)
    j = lax.broadcasted_iota(jnp.int32, cos_ref.shape, 1).astype(F32)
    inv = jnp.exp(j * (-math.log(ROPE_BASE) / half))
    ang = pos * inv
    cos_ref[...] = jnp.cos(ang)
    sin_ref[...] = jnp.sin(ang)


def _rope_table(S):
    half = RET_KDIM // 2
    shape = jax.ShapeDtypeStruct((S, half), F32)
    return pl.pallas_call(_rope_table_kernel, out_shape=[shape, shape], name="rope_table")()


def _retproj_kernel(x_ref, g_ref, wq_ref, wk_ref, wv_ref, wg_ref, cos_ref, sin_ref,
                    q_ref, k_ref, v_ref, gate_ref):
    xn = _rms(x_ref[...], g_ref[...]).astype(BF16)
    cos = cos_ref[...]
    sin = sin_ref[...]
    half = RET_KDIM // 2

    def rotary_store(t, dst_ref):
        for h in range(RET_HEADS):
            x1 = t[:, h * RET_KDIM:h * RET_KDIM + half]
            x2 = t[:, h * RET_KDIM + half:(h + 1) * RET_KDIM]
            dst_ref[:, h * RET_KDIM:h * RET_KDIM + half] = (x1 * cos - x2 * sin).astype(BF16)
            dst_ref[:, h * RET_KDIM + half:(h + 1) * RET_KDIM] = (x1 * sin + x2 * cos).astype(BF16)

    rotary_store(jnp.dot(xn, wq_ref[...], preferred_element_type=F32), q_ref)
    rotary_store(jnp.dot(xn, wk_ref[...], preferred_element_type=F32) * (RET_KDIM ** -0.5), k_ref)
    v_ref[...] = jnp.dot(xn, wv_ref[...], preferred_element_type=F32).astype(BF16)
    gate_ref[...] = _silu(jnp.dot(xn, wg_ref[...], preferred_element_type=F32)).astype(BF16)


def _retproj(x, g, wq, wk, wv, wg, cos, sin):
    B, S, D = x.shape
    tm = ROW_TILE
    kw, vw = RET_HEADS * RET_KDIM, RET_HEADS * RET_VDIM
    table = pl.BlockSpec((tm, RET_KDIM // 2), lambda b, i: (i, 0))
    return pl.pallas_call(
        _retproj_kernel,
        grid=(B, S // tm),
        in_specs=[_rows(tm, D), _resident((1, D)), _resident(wq.shape), _resident(wk.shape),
                  _resident(wv.shape), _resident(wg.shape), table, table],
        out_specs=[_rows(tm, kw), _rows(tm, kw), _rows(tm, vw), _rows(tm, vw)],
        out_shape=[jax.ShapeDtypeStruct((B, S, kw), BF16), jax.ShapeDtypeStruct((B, S, kw), BF16),
                   jax.ShapeDtypeStruct((B, S, vw), BF16), jax.ShapeDtypeStruct((B, S, vw), BF16)],
        compiler_params=_params("parallel", "parallel"),
        name="ret_proj",
    )(x, g, wq, wk, wv, wg, cos, sin)


def _retention_kernel(q_ref, k_ref, v_ref, gate_ref, o_ref, state_ref, decay_ref):
    C = RET_CHUNK
    log_gs = [math.log1p(-(2.0 ** (-5.0 - h))) for h in range(RET_HEADS)]

    @pl.when((pl.program_id(0) == 0) & (pl.program_id(1) == 0))
    def _():
        ii = lax.broadcasted_iota(jnp.int32, (C, C), 0)
        jj = lax.broadcasted_iota(jnp.int32, (C, C), 1)
        diff = (ii - jj).astype(F32)
        for h in range(RET_HEADS):
            decay_ref[h] = jnp.where(diff >= 0, jnp.exp(log_gs[h] * jnp.maximum(diff, 0.0)), 0.0)

    @pl.when(pl.program_id(1) == 0)
    def _():
        state_ref[...] = jnp.zeros_like(state_ref)

    pos = lax.broadcasted_iota(jnp.int32, (C, 1), 0).astype(F32)

    for h in range(RET_HEADS):
        log_g = log_gs[h]
        inner_decay = decay_ref[h]
        q_decay = jnp.exp(log_g * (pos + 1.0))
        k_decay = jnp.exp(log_g * (C - 1.0 - pos))
        chunk_decay = math.exp(log_g * C)
        kcols = slice(h * RET_KDIM, (h + 1) * RET_KDIM)
        vcols = slice(h * RET_VDIM, (h + 1) * RET_VDIM)
        for c in range(q_ref.shape[0] // C):
            rows = slice(c * C, (c + 1) * C)
            qc = q_ref[rows, kcols]
            kc = k_ref[rows, kcols]
            vc = v_ref[rows, vcols]
            state = state_ref[h]
            scores = lax.dot_general(qc, kc, (((1,), (1,)), ((), ())),
                                     preferred_element_type=F32) * inner_decay
            o = (jnp.dot(scores.astype(BF16), vc, preferred_element_type=F32)
                 + jnp.dot(qc, state.astype(BF16), preferred_element_type=F32) * q_decay)
            kd = (kc.astype(F32) * k_decay).astype(BF16)
            state_ref[h] = state * chunk_decay + lax.dot_general(
                kd, vc, (((0,), (0,)), ((), ())), preferred_element_type=F32)
            mu = jnp.mean(o, axis=-1, keepdims=True)
            var = jnp.mean(jnp.square(o - mu), axis=-1, keepdims=True)
            on = (o - mu) * lax.rsqrt(var + EPS)
            o_ref[rows, vcols] = (gate_ref[rows, vcols].astype(F32) * on).astype(BF16)


def _retention(q, k, v, gate):
    B, S, kw = q.shape
    vw = v.shape[-1]
    tc = RET_TILE
    return pl.pallas_call(
        _retention_kernel,
        grid=(B, S // tc),
        in_specs=[_rows(tc, kw), _rows(tc, kw), _rows(tc, vw), _rows(tc, vw)],
        out_specs=_rows(tc, vw),
        out_shape=jax.ShapeDtypeStruct((B, S, vw), BF16),
        scratch_shapes=[pltpu.VMEM((RET_HEADS, RET_KDIM, RET_VDIM), F32),
                        pltpu.VMEM((RET_HEADS, RET_CHUNK, RET_CHUNK), F32)],
        compiler_params=_params("arbitrary", "arbitrary"),
        name="retention",
    )(q, k, v, gate)


def _proj_residual_kernel(x_ref, a_ref, w_ref, o_ref):
    o_ref[...] = x_ref[...] + jnp.dot(a_ref[...], w_ref[...], preferred_element_type=F32)


def _proj_residual(x, a, w):
    B, S, D = x.shape
    tm = ROW_TILE
    return pl.pallas_call(
        _proj_residual_kernel,
        grid=(B, S // tm),
        in_specs=[_rows(tm, D), _rows(tm, a.shape[-1]), _resident(w.shape)],
        out_specs=_rows(tm, D),
        out_shape=jax.ShapeDtypeStruct((B, S, D), F32),
        compiler_params=_params("parallel", "parallel"),
        name="ret_outproj",
    )(x, a, w)


def _head_sum_matrix():
    g = jnp.arange(A_WIDTH // 2) // A_HEAD_DIM
    return (g[:, None] == g[None, :]).astype(BF16)


def _head_expand_matrix():
    head_of_lane = jnp.arange(A_WIDTH) // A_HEAD_DIM
    return (jnp.arange(LANES)[:, None] == head_of_lane[None, :]).astype(BF16)


def kernel(x, even_norm, even_w_in, even_q_gain, even_k_gain, even_sconv_w, even_w_out, odd_norm,
           ret_wq, ret_wk, ret_wv, ret_wg, ret_gn_gain, ret_wo, ffn_norm, ffn_w_up, ffn_conv_w,
           ffn_w_down):
    B, S, D = x.shape
    depth = ffn_norm.shape[0]
    bf = lambda w: w.astype(BF16)
    row = lambda v: v.reshape(1, -1).astype(F32)
    hsum = _head_sum_matrix()
    expand = _head_expand_matrix()
    cos = sin = None

    for l in range(depth):
        if l % 2 == 0:
            e = l // 2
            qs, ks, vs, b = _inproj(x, row(even_norm[e]), bf(even_w_in[e]),
                                    row(jnp.tile(even_q_gain[e], A_HEADS) * _Q_SCALE),
                                    row(jnp.tile(even_k_gain[e], A_HEADS)),
                                    even_sconv_w[e], hsum)
            os, lses = zip(*[_dilated_attention_one(q, k, v, d)
                             for q, k, v, d in zip(qs, ks, vs, DILATIONS)])
            x = _outproj(x, os, lses, b, expand, bf(even_w_out[e]))
        else:
            o = l // 2
            if cos is None:
                cos, sin = _rope_table(S)
            q, k, v, gate = _retproj(x, row(odd_norm[o]), bf(ret_wq[o]), bf(ret_wk[o]),
                                     bf(ret_wv[o]), bf(ret_wg[o]), cos, sin)
            a = _retention(q, k, v, gate)
            x = _proj_residual(x, a, bf(ret_gn_gain[o].reshape(-1, 1) * ret_wo[o]))
        x = _ffn(x, row(ffn_norm[l]), bf(ffn_w_up[l]), ffn_conv_w[l], bf(ffn_w_down[l]))
    return x
```

```python
import functools
import math

import jax
import jax.numpy as jnp
from jax import lax
from jax.experimental import pallas as pl
from jax.experimental.pallas import tpu as pltpu

F32 = jnp.float32
BF16 = jnp.bfloat16

D_MODEL = 1024
A_HEADS = 8
A_HEAD_DIM = 64
A_WIDTH = A_HEADS * A_HEAD_DIM
B_WIDTH = D_MODEL - A_WIDTH
DILATIONS = (1, 4, 16)
REACH = 128
RET_HEADS = 4
RET_KDIM = D_MODEL // RET_HEADS
RET_VDIM = 2 * D_MODEL // RET_HEADS
RET_CHUNK = 256
ROPE_BASE = 10000.0
D_FF = 2816
EPS = 1e-6

SUBLANES = 8
LANES = 128
VMEM_LIMIT_BYTES = 56 * 1024 * 1024

ROW_TILE = 1024
FFN_ROW_TILE = 1024
FF_TILE = 256
ATTN_ROWS = 2048
RET_TILE = 512


def _params(*semantics):
    return pltpu.CompilerParams(dimension_semantics=semantics, vmem_limit_bytes=VMEM_LIMIT_BYTES)


def _resident(shape):
    nd = len(shape)
    return pl.BlockSpec(shape, lambda *_: (0,) * nd, pipeline_mode=pl.Buffered(1))


def _rows(tm, width):
    return pl.BlockSpec((None, tm, width), lambda b, i: (b, i, 0))


def _stream_rows(dil, tm, width):
    return pl.BlockSpec((None, dil, tm // dil, width), lambda b, i: (b, 0, i, 0))


def _rms(x, g):
    ms = jnp.mean(x * x, axis=-1, keepdims=True)
    return x * lax.rsqrt(ms + EPS) * g


def _silu(g):
    return g / (1.0 + jnp.exp2(g * -math.log2(math.e)))


def _shift_rows(cur, prev, k):
    rolled = pltpu.roll(cur, k, axis=0)
    prev_rolled = pltpu.roll(prev, k, axis=0)
    row = lax.broadcasted_iota(jnp.int32, prev.shape, 0)
    head = jnp.where(row < k, prev_rolled, rolled[:SUBLANES])
    return jnp.concatenate([head, rolled[SUBLANES:]], axis=0)


def _causal_conv3(cur, prev, w):
    return (_shift_rows(cur, prev, 2) * w[0:1] + _shift_rows(cur, prev, 1) * w[1:2]
            + cur * w[2:3])


def _stream_group(dil):
    return max(LANES, 2 * SUBLANES * dil)


def _stream_perm_matrix(dil):
    group = _stream_group(dil)
    dst = jnp.arange(group)
    src = (dst % (group // dil)) * dil + dst // (group // dil)
    return (jnp.arange(group)[None, :] == src[:, None]).astype(BF16)


def _store_streams(t, perm_refs, out_refs):
    tm = t.shape[0]
    tb = t.astype(BF16)
    for dil, perm_ref, out_ref in zip(DILATIONS, perm_refs, out_refs):
        if dil == 1:
            out_ref[0] = tb
            continue
        group = perm_ref.shape[0]
        n = group // dil
        for g in range(tm // group):
            y = jnp.dot(perm_ref[...], tb[g * group:(g + 1) * group, :],
                        preferred_element_type=F32).astype(BF16)
            for r in range(dil):
                out_ref[r, g * n:(g + 1) * n, :] = y[r * n:(r + 1) * n, :]


def _inproj_kernel(x_ref, g_ref, w_ref, qg_ref, kg_ref, cw_ref, hsum_ref, *rest):
    nd = len(DILATIONS)
    perm_refs, rest = rest[:nd], rest[nd:]
    q_refs, k_refs, v_refs = rest[:nd], rest[nd:2 * nd], rest[2 * nd:3 * nd]
    b_ref, halo_ref = rest[3 * nd:]

    @pl.when(pl.program_id(1) == 0)
    def _():
        halo_ref[...] = jnp.zeros_like(halo_ref)

    xn = _rms(x_ref[...], g_ref[...]).astype(BF16)

    def proj(idx):
        return jnp.dot(xn, w_ref[:, idx * A_WIDTH:(idx + 1) * A_WIDTH], preferred_element_type=F32)

    def head_norm(t, gain):
        t2 = (t * t).astype(BF16)
        half = A_WIDTH // 2
        ss = jnp.concatenate(
            [jnp.dot(t2[:, :half], hsum_ref[...], preferred_element_type=F32),
             jnp.dot(t2[:, half:], hsum_ref[...], preferred_element_type=F32)], axis=1)
        return t * lax.rsqrt(ss * (1.0 / A_HEAD_DIM) + EPS) * gain

    _store_streams(head_norm(proj(0), qg_ref[...]), perm_refs, q_refs)
    _store_streams(head_norm(proj(1), kg_ref[...]), perm_refs, k_refs)
    _store_streams(proj(2), perm_refs, v_refs)
    gate_b = proj(3)
    s = proj(4) * proj(5)
    conv = _causal_conv3(s, halo_ref[...], cw_ref[...])
    halo_ref[...] = s[s.shape[0] - SUBLANES:]
    b_ref[...] = (gate_b * conv).astype(BF16)


def _inproj(x, g, w_in, q_gain, k_gain, sconv_w, hsum):
    B, S, D = x.shape
    tm = ROW_TILE
    streams = [jax.ShapeDtypeStruct((B, d, S // d, A_WIDTH), BF16) for d in DILATIONS]
    stream_specs = [_stream_rows(d, tm, A_WIDTH) for d in DILATIONS]
    perms = [_stream_perm_matrix(d) for d in DILATIONS]
    outs = pl.pallas_call(
        _inproj_kernel,
        grid=(B, S // tm),
        in_specs=[_rows(tm, D), _resident((1, D)), _resident(w_in.shape), _resident((1, A_WIDTH)),
                  _resident((1, A_WIDTH)), _resident(sconv_w.shape), _resident(hsum.shape)]
                 + [_resident(p.shape) for p in perms],
        out_specs=stream_specs * 3 + [_rows(tm, B_WIDTH)],
        out_shape=streams * 3 + [jax.ShapeDtypeStruct((B, S, B_WIDTH), BF16)],
        scratch_shapes=[pltpu.VMEM((SUBLANES, B_WIDTH), F32)],
        compiler_params=_params("parallel", "arbitrary"),
        name="even_inproj",
    )(x, g, w_in, q_gain, k_gain, sconv_w, hsum, *perms)
    nd = len(DILATIONS)
    return outs[:nd], outs[nd:2 * nd], outs[2 * nd:3 * nd], outs[3 * nd]


_Q_SCALE = A_HEAD_DIM ** -0.5 * math.log2(math.e)


def _col_reduce(x, op):
    parts = x.reshape(4, x.shape[0] // 4, x.shape[1])
    part = op(op(parts[0], parts[1]), op(parts[2], parts[3]))
    reduce = jnp.max if op is jnp.maximum else jnp.sum
    return reduce(part, axis=0, keepdims=True)


def _attn_kernel(q_ref, kp_ref, kc_ref, vp_ref, vc_ref, o_ref, lse_ref,
                 k_scr, vwin_scr, bias_scr, s_scr, ot_scr, lse_scr):
    first_tile = pl.program_id(2) == 0
    hd = A_HEAD_DIM
    n_pairs = A_WIDTH // LANES
    n_streams, tq = q_ref.shape[0], q_ref.shape[1]
    blocks_per_stream = tq // REACH
    n_blocks = n_streams * blocks_per_stream
    assert blocks_per_stream & (blocks_per_stream - 1) == 0 and n_blocks % 2 == 0

    kj = lax.broadcasted_iota(jnp.int32, (2 * REACH, 2 * REACH), 0)
    qi = lax.broadcasted_iota(jnp.int32, (2 * REACH, 2 * REACH), 1) % REACH
    band = (kj >= qi) & (kj <= qi + REACH)
    bias_scr[0] = jnp.where(band & ((kj >= REACH) | jnp.logical_not(first_tile)), 0.0, -1e30)
    bias_scr[1] = jnp.where(band, 0.0, -1e30)
    for s in range(n_streams):
        v_t = jnp.concatenate([vp_ref[s], vc_ref[s]], axis=0).T
        for j in range(blocks_per_stream):
            vwin_scr[s * blocks_per_stream + j] = v_t[:, j * REACH:(j + 2) * REACH]
        for pr in range(n_pairs):
            slab = slice(pr * LANES, (pr + 1) * LANES)
            k_scr[pr, s, 0:REACH, :] = kp_ref[s, :, slab]
            k_scr[pr, s, REACH:, :] = kc_ref[s, :, slab]
    low_half = lax.broadcasted_iota(jnp.int32, (REACH, LANES), 1) < hd
    zero = jnp.zeros((), BF16)
    ones_rows = jnp.ones((2 * SUBLANES, 2 * REACH), BF16)

    def block_scores(blk, slot):
        s = lax.shift_right_logical(blk, blocks_per_stream.bit_length() - 1)
        j = blk & (blocks_per_stream - 1)
        bias = bias_scr[jnp.minimum(j, 1)]
        maxima = []
        for pr in range(n_pairs):
            k_win = k_scr[pr, s, pl.ds(pl.multiple_of(j * REACH, REACH), 2 * REACH), :]
            q_slab = q_ref[s, pl.ds(pl.multiple_of(j * REACH, REACH), REACH),
                           pr * LANES:(pr + 1) * LANES]
            q_two = jnp.concatenate([jnp.where(low_half, q_slab, zero),
                                     jnp.where(low_half, zero, q_slab)], axis=0)
            sc = lax.dot_general(k_win, q_two, (((1,), (1,)), ((), ())),
                                 preferred_element_type=F32) + bias
            s_scr[slot, pr] = sc
            maxima.append(_col_reduce(sc, jnp.maximum))
        return tuple(maxima)

    def block_finish(blk, slot, maxima):
        for pr in range(n_pairs):
            p_t = jnp.exp2(s_scr[slot, pr] - maxima[pr]).astype(BF16)
            v_pair = vwin_scr[blk, pr * LANES:(pr + 1) * LANES, :]
            o_aug = jnp.dot(jnp.concatenate([v_pair, ones_rows], axis=0), p_t,
                            preferred_element_type=F32)
            den = o_aug[2 * hd:2 * hd + 1, :]
            o_t = o_aug[:2 * hd, :] / den
            ot_scr[blk, pr * LANES:(pr + 1) * LANES, :] = jnp.concatenate(
                [o_t[:hd, :REACH], o_t[hd:, REACH:]], axis=0).astype(BF16)
            lse = (maxima[pr] + jnp.log2(den)) * math.log(2.0)
            lse_scr[blk, pr] = jnp.broadcast_to(lse, (SUBLANES, 2 * REACH))

    def two_blocks(u, maxima_even, lookahead=True):
        blk = 2 * u
        maxima_odd = block_scores(blk + 1, 1)
        block_finish(blk, 0, maxima_even)
        maxima_next = block_scores(blk + 2, 0) if lookahead else None
        block_finish(blk + 1, 1, maxima_odd)
        return maxima_next

    maxima = lax.fori_loop(0, n_blocks // 2 - 1, two_blocks, block_scores(jnp.int32(0), 0))
    two_blocks(jnp.int32(n_blocks // 2 - 1), maxima, lookahead=False)

    for blk in range(n_blocks):
        s, j = divmod(blk, blocks_per_stream)
        rows = slice(j * REACH, (j + 1) * REACH)
        o_ref[s, rows, :] = ot_scr[blk].T
        lse_rows = []
        for pr in range(n_pairs):
            lse_rows += [lse_scr[blk, pr, 0:1, :REACH], lse_scr[blk, pr, 0:1, REACH:]]
        lse_rows.append(jnp.zeros((LANES - A_HEADS, REACH), F32))
        lse_ref[s, rows, :] = jnp.concatenate(lse_rows, axis=0).T


def _dilated_attention_one(q, k, v, dil):
    B, _, L, W = q.shape
    tq = min(ATTN_ROWS, L)
    ns = min(ATTN_ROWS // tq, dil)
    n_blocks, n_pairs = ns * tq // REACH, W // LANES
    cur = pl.BlockSpec((None, ns, tq, W), lambda b, r, i: (b, r, i, 0))
    prev = pl.BlockSpec((None, ns, REACH, W),
                        lambda b, r, i: (b, r, jnp.maximum(i * (tq // REACH) - 1, 0), 0))
    return pl.pallas_call(
        _attn_kernel,
        grid=(B, dil // ns, L // tq),
        in_specs=[cur, prev, cur, prev, cur],
        out_specs=[cur, pl.BlockSpec((None, ns, tq, LANES), lambda b, r, i: (b, r, i, 0))],
        out_shape=[jax.ShapeDtypeStruct((B, dil, L, W), BF16),
                   jax.ShapeDtypeStruct((B, dil, L, LANES), F32)],
        scratch_shapes=[
            pltpu.VMEM((n_pairs, ns, REACH + tq, LANES), BF16),
            pltpu.VMEM((n_blocks, W, 2 * REACH), BF16),
            pltpu.VMEM((2, 2 * REACH, 2 * REACH), F32),
            pltpu.VMEM((2, n_pairs, 2 * REACH, 2 * REACH), F32),
            pltpu.VMEM((n_blocks, W, REACH), BF16),
            pltpu.VMEM((n_blocks, n_pairs, SUBLANES, 2 * REACH), F32),
        ],
        compiler_params=_params("parallel", "parallel", "arbitrary"),
        name=f"dilated_attn_d{dil}",
    )(q, k, k, v, v)


def _load_streams(src_ref, dil, slab_ref):
    n_slabs = src_ref.shape[-1] // LANES
    if dil == 1:
        t = src_ref[0].astype(F32)
        return [t[:, s * LANES:(s + 1) * LANES] for s in range(n_slabs)]
    n = src_ref.shape[1]
    for r in range(dil):
        for s in range(n_slabs):
            slab_ref[s, pl.ds(r, n, stride=dil), :] = (
                src_ref[r, :, s * LANES:(s + 1) * LANES].astype(F32))
    return [slab_ref[s] for s in range(n_slabs)]


def _outproj_kernel(x_ref, *rest):
    nd = len(DILATIONS)
    o_refs, l_refs = rest[:nd], rest[nd:2 * nd]
    b_ref, expand_ref, w_ref, out_ref, o_slab_ref, l_slab_ref = rest[2 * nd:]
    n_slabs = A_WIDTH // LANES

    ls = [_load_streams(l_ref, d, l_slab_ref.at[n])[0]
          for n, (d, l_ref) in enumerate(zip(DILATIONS, l_refs))]
    mx = functools.reduce(jnp.maximum, ls)
    es = [jnp.exp(l - mx) for l in ls]
    tot = functools.reduce(jnp.add, es)
    a_slabs = [None] * n_slabs
    for n, (d, e, o_ref) in enumerate(zip(DILATIONS, es, o_refs)):
        w = e / tot
        wf = jnp.dot(w.astype(BF16), expand_ref[...], preferred_element_type=F32)
        o_slabs = _load_streams(o_ref, d, o_slab_ref.at[n])
        for s in range(n_slabs):
            term = wf[:, s * LANES:(s + 1) * LANES] * o_slabs[s]
            a_slabs[s] = term if a_slabs[s] is None else a_slabs[s] + term
    a = jnp.concatenate(a_slabs, axis=1).astype(BF16)
    y = (jnp.dot(a, w_ref[:A_WIDTH, :], preferred_element_type=F32)
         + jnp.dot(b_ref[...], w_ref[A_WIDTH:, :], preferred_element_type=F32))
    out_ref[...] = x_ref[...] + y


def _outproj(x, os, lses, b, expand, w_out):
    B, S, D = x.shape
    tm = ROW_TILE
    nd = len(DILATIONS)
    return pl.pallas_call(
        _outproj_kernel,
        grid=(B, S // tm),
        in_specs=[_rows(tm, D)] + [_stream_rows(d, tm, A_WIDTH) for d in DILATIONS]
                 + [_stream_rows(d, tm, LANES) for d in DILATIONS]
                 + [_rows(tm, B_WIDTH), _resident(expand.shape), _resident(w_out.shape)],
        out_specs=_rows(tm, D),
        out_shape=jax.ShapeDtypeStruct((B, S, D), F32),
        scratch_shapes=[pltpu.VMEM((nd, A_WIDTH // LANES, tm, LANES), F32),
                        pltpu.VMEM((nd, 1, tm, LANES), F32)],
        compiler_params=_params("parallel", "parallel"),
        name="even_outproj",
    )(x, *os, *lses, b, expand, w_out)


def _ffn_kernel(x_ref, g_ref, wup_ref, cw_ref, wdn_ref, o_ref, h_ref, halo_ref):
    @pl.when(pl.program_id(1) == 0)
    def _():
        halo_ref[...] = jnp.zeros_like(halo_ref)

    x = x_ref[...]
    tm = x.shape[0]
    xn = _rms(x, g_ref[...]).astype(BF16)

    def conv_up(col):
        cols = slice(col, col + FF_TILE)
        u = jnp.dot(xn, wup_ref[:, cols], preferred_element_type=F32)
        y = _causal_conv3(u, halo_ref[:, cols], cw_ref[:, cols])
        halo_ref[:, cols] = u[tm - SUBLANES:]
        return y

    for c in range(D_FF // FF_TILE):
        gate = conv_up(c * FF_TILE)
        val = conv_up(D_FF + c * FF_TILE)
        h_ref[:, c * FF_TILE:(c + 1) * FF_TILE] = (_silu(gate) * val).astype(BF16)
    o_ref[...] = x + jnp.dot(h_ref[...], wdn_ref[...], preferred_element_type=F32)


def _ffn(x, g, w_up, conv_w, w_down):
    B, S, D = x.shape
    tm = FFN_ROW_TILE
    return pl.pallas_call(
        _ffn_kernel,
        grid=(B, S // tm),
        in_specs=[_rows(tm, D), _resident((1, D)), _resident(w_up.shape), _resident(conv_w.shape),
                  _resident(w_down.shape)],
        out_specs=_rows(tm, D),
        out_shape=jax.ShapeDtypeStruct((B, S, D), F32),
        scratch_shapes=[pltpu.VMEM((tm, D_FF), BF16), pltpu.VMEM((SUBLANES, 2 * D_FF), F32)],
        compiler_params=_params("parallel", "arbitrary"),
        name="conv_glu_ffn",
    )(x, g, w_up, conv_w, w_down)


def _rope_table_kernel(cos_ref, sin_ref):
    half = cos_ref.shape[1]
    pos = lax.broadcasted_iota(jnp.int32, cos_ref.shape, 0).astype(F32)
    j = lax.broadcasted_iota(jnp.int32, cos_ref.shape, 1).astype(F32)
    inv = jnp.exp(j * (-math.log(ROPE_BASE) / half))
    ang = pos * inv
    cos_ref[...] = jnp.cos(ang)
    sin_ref[...] = jnp.sin(ang)


def _rope_table(S):
    half = RET_KDIM // 2
    shape = jax.ShapeDtypeStruct((S, half), F32)
    return pl.pallas_call(_rope_table_kernel, out_shape=[shape, shape], name="rope_table")()


def _retproj_kernel(x_ref, g_ref, wq_ref, wk_ref, wv_ref, wg_ref, cos_ref, sin_ref,
                    q_ref, k_ref, v_ref, gate_ref):
    xn = _rms(x_ref[...], g_ref[...]).astype(BF16)
    cos = cos_ref[...]
    sin = sin_ref[...]
    half = RET_KDIM // 2

    def rotary_store(t, dst_ref):
        for h in range(RET_HEADS):
            x1 = t[:, h * RET_KDIM:h * RET_KDIM + half]
            x2 = t[:, h * RET_KDIM + half:(h + 1) * RET_KDIM]
            dst_ref[:, h * RET_KDIM:h * RET_KDIM + half] = (x1 * cos - x2 * sin).astype(BF16)
            dst_ref[:, h * RET_KDIM + half:(h + 1) * RET_KDIM] = (x1 * sin + x2 * cos).astype(BF16)

    rotary_store(jnp.dot(xn, wq_ref[...], preferred_element_type=F32), q_ref)
    rotary_store(jnp.dot(xn, wk_ref[...], preferred_element_type=F32) * (RET_KDIM ** -0.5), k_ref)
    v_ref[...] = jnp.dot(xn, wv_ref[...], preferred_element_type=F32).astype(BF16)
    gate_ref[...] = _silu(jnp.dot(xn, wg_ref[...], preferred_element_type=F32)).astype(BF16)


def _retproj(x, g, wq, wk, wv, wg, cos, sin):
    B, S, D = x.shape
    tm = ROW_TILE
    kw, vw = RET_HEADS * RET_KDIM, RET_HEADS * RET_VDIM
    table = pl.BlockSpec((tm, RET_KDIM // 2), lambda b, i: (i, 0))
    return pl.pallas_call(
        _retproj_kernel,
        grid=(B, S // tm),
        in_specs=[_rows(tm, D), _resident((1, D)), _resident(wq.shape), _resident(wk.shape),
                  _resident(wv.shape), _resident(wg.shape), table, table],
        out_specs=[_rows(tm, kw), _rows(tm, kw), _rows(tm, vw), _rows(tm, vw)],
        out_shape=[jax.ShapeDtypeStruct((B, S, kw), BF16), jax.ShapeDtypeStruct((B, S, kw), BF16),
                   jax.ShapeDtypeStruct((B, S, vw), BF16), jax.ShapeDtypeStruct((B, S, vw), BF16)],
        compiler_params=_params("parallel", "parallel"),
        name="ret_proj",
    )(x, g, wq, wk, wv, wg, cos, sin)


def _retention_kernel(x_ref, q_ref, k_ref, v_ref, gate_ref, wo_ref, o_ref, state_ref, decay_ref,
                      a_ref):
    C = RET_CHUNK
    log_gs = [math.log1p(-(2.0 ** (-5.0 - h))) for h in range(RET_HEADS)]

    @pl.when((pl.program_id(0) == 0) & (pl.program_id(1) == 0))
    def _():
        ii = lax.broadcasted_iota(jnp.int32, (C, C), 0)
        jj = lax.broadcasted_iota(jnp.int32, (C, C), 1)
        diff = (ii - jj).astype(F32)
        for h in range(RET_HEADS):
            decay_ref[h] = jnp.where(diff >= 0, jnp.exp(log_gs[h] * jnp.maximum(diff, 0.0)), 0.0)

    @pl.when(pl.program_id(1) == 0)
    def _():
        state_ref[...] = jnp.zeros_like(state_ref)

    pos = lax.broadcasted_iota(jnp.int32, (C, 1), 0).astype(F32)
    q_decays = [jnp.exp(lg * (pos + 1.0)) for lg in log_gs]
    k_decays = [jnp.exp(lg * (C - 1.0 - pos)) for lg in log_gs]

    for c in range(q_ref.shape[0] // C):
        rows = slice(c * C, (c + 1) * C)
        for h in range(RET_HEADS):
            kcols = slice(h * RET_KDIM, (h + 1) * RET_KDIM)
            vcols = slice(h * RET_VDIM, (h + 1) * RET_VDIM)
            qc = q_ref[rows, kcols]
            kc = k_ref[rows, kcols]
            vc = v_ref[rows, vcols]
            state = state_ref[h]
            scores = lax.dot_general(qc, kc, (((1,), (1,)), ((), ())),
                                     preferred_element_type=F32) * decay_ref[h]
            o = (jnp.dot(scores.astype(BF16), vc, preferred_element_type=F32)
                 + jnp.dot(qc, state.astype(BF16), preferred_element_type=F32) * q_decays[h])
            kd = (kc.astype(F32) * k_decays[h]).astype(BF16)
            state_ref[h] = state * math.exp(log_gs[h] * C) + lax.dot_general(
                kd, vc, (((0,), (0,)), ((), ())), preferred_element_type=F32)
            mu = jnp.mean(o, axis=-1, keepdims=True)
            var = jnp.mean(jnp.square(o - mu), axis=-1, keepdims=True)
            on = (o - mu) * lax.rsqrt(var + EPS)
            a_ref[rows, vcols] = (gate_ref[rows, vcols].astype(F32) * on).astype(BF16)
        o_ref[rows, :] = x_ref[rows, :] + jnp.dot(a_ref[rows, :], wo_ref[...],
                                                  preferred_element_type=F32)


def _retention(x, q, k, v, gate, wo):
    B, S, D = x.shape
    kw, vw = q.shape[-1], v.shape[-1]
    tc = RET_TILE
    return pl.pallas_call(
        _retention_kernel,
        grid=(B, S // tc),
        in_specs=[_rows(tc, D), _rows(tc, kw), _rows(tc, kw), _rows(tc, vw), _rows(tc, vw),
                  _resident(wo.shape)],
        out_specs=_rows(tc, D),
        out_shape=jax.ShapeDtypeStruct((B, S, D), F32),
        scratch_shapes=[pltpu.VMEM((RET_HEADS, RET_KDIM, RET_VDIM), F32),
                        pltpu.VMEM((RET_HEADS, RET_CHUNK, RET_CHUNK), F32),
                        pltpu.VMEM((tc, vw), BF16)],
        compiler_params=_params("arbitrary", "arbitrary"),
        name="retention",
    )(x, q, k, v, gate, wo)


def _head_sum_matrix():
    g = jnp.arange(A_WIDTH // 2) // A_HEAD_DIM
    return (g[:, None] == g[None, :]).astype(BF16)


def _head_expand_matrix():
    head_of_lane = jnp.arange(A_WIDTH) // A_HEAD_DIM
    return (jnp.arange(LANES)[:, None] == head_of_lane[None, :]).astype(BF16)


def kernel(x, even_norm, even_w_in, even_q_gain, even_k_gain, even_sconv_w, even_w_out, odd_norm,
           ret_wq, ret_wk, ret_wv, ret_wg, ret_gn_gain, ret_wo, ffn_norm, ffn_w_up, ffn_conv_w,
           ffn_w_down):
    B, S, D = x.shape
    depth = ffn_norm.shape[0]
    bf = lambda w: w.astype(BF16)
    row = lambda v: v.reshape(1, -1).astype(F32)
    hsum = _head_sum_matrix()
    expand = _head_expand_matrix()
    cos = sin = None

    for l in range(depth):
        if l % 2 == 0:
            e = l // 2
            qs, ks, vs, b = _inproj(x, row(even_norm[e]), bf(even_w_in[e]),
                                    row(jnp.tile(even_q_gain[e], A_HEADS) * _Q_SCALE),
                                    row(jnp.tile(even_k_gain[e], A_HEADS)),
                                    even_sconv_w[e], hsum)
            os, lses = zip(*[_dilated_attention_one(q, k, v, d)
                             for q, k, v, d in zip(qs, ks, vs, DILATIONS)])
            x = _outproj(x, os, lses, b, expand, bf(even_w_out[e]))
        else:
            o = l // 2
            if cos is None:
                cos, sin = _rope_table(S)
            q, k, v, gate = _retproj(x, row(odd_norm[o]), bf(ret_wq[o]), bf(ret_wk[o]),
                                     bf(ret_wv[o]), bf(ret_wg[o]), cos, sin)
            x = _retention(x, q, k, v, gate, bf(ret_gn_gain[o].reshape(-1, 1) * ret_wo[o]))
        x = _ffn(x, row(ffn_norm[l]), bf(ffn_w_up[l]), ffn_conv_w[l], bf(ffn_w_down[l]))
    return x
```

```python
import functools
import math

import jax
import jax.numpy as jnp
from jax import lax
from jax.experimental import pallas as pl
from jax.experimental.pallas import tpu as pltpu

F32 = jnp.float32
BF16 = jnp.bfloat16

D_MODEL = 1024
A_HEADS = 8
A_HEAD_DIM = 64
A_WIDTH = A_HEADS * A_HEAD_DIM
B_WIDTH = D_MODEL - A_WIDTH
DILATIONS = (1, 4, 16)
REACH = 128
RET_HEADS = 4
RET_KDIM = D_MODEL // RET_HEADS
RET_VDIM = 2 * D_MODEL // RET_HEADS
RET_CHUNK = 256
ROPE_BASE = 10000.0
D_FF = 2816
EPS = 1e-6

SUBLANES = 8
LANES = 128
VMEM_LIMIT_BYTES = 56 * 1024 * 1024

ROW_TILE = 1024
FFN_ROW_TILE = 1024
FF_TILE = 256
ATTN_ROWS = 2048
RET_TILE = 512


def _params(*semantics):
    return pltpu.CompilerParams(dimension_semantics=semantics, vmem_limit_bytes=VMEM_LIMIT_BYTES)


def _resident(shape):
    nd = len(shape)
    return pl.BlockSpec(shape, lambda *_: (0,) * nd, pipeline_mode=pl.Buffered(1))


def _rows(tm, width):
    return pl.BlockSpec((None, tm, width), lambda b, i: (b, i, 0))


def _stream_rows(dil, tm, width):
    return pl.BlockSpec((None, dil, tm // dil, width), lambda b, i: (b, 0, i, 0))


def _rms(x, g):
    ms = jnp.mean(x * x, axis=-1, keepdims=True)
    return x * lax.rsqrt(ms + EPS) * g


def _silu(g):
    return g / (1.0 + jnp.exp2(g * -math.log2(math.e)))


def _shift_rows(cur, prev, k):
    rolled = pltpu.roll(cur, k, axis=0)
    prev_rolled = pltpu.roll(prev, k, axis=0)
    row = lax.broadcasted_iota(jnp.int32, prev.shape, 0)
    head = jnp.where(row < k, prev_rolled, rolled[:SUBLANES])
    return jnp.concatenate([head, rolled[SUBLANES:]], axis=0)


def _causal_conv3(cur, prev, w):
    return (_shift_rows(cur, prev, 2) * w[0:1] + _shift_rows(cur, prev, 1) * w[1:2]
            + cur * w[2:3])


def _stream_group(dil):
    return max(LANES, 2 * SUBLANES * dil)


def _stream_perm_matrix(dil):
    group = _stream_group(dil)
    dst = jnp.arange(group)
    src = (dst % (group // dil)) * dil + dst // (group // dil)
    return (jnp.arange(group)[None, :] == src[:, None]).astype(BF16)


def _store_streams(t, perm_refs, out_refs):
    tm = t.shape[0]
    tb = t.astype(BF16)
    for dil, perm_ref, out_ref in zip(DILATIONS, perm_refs, out_refs):
        if dil == 1:
            out_ref[0] = tb
            continue
        group = perm_ref.shape[0]
        n = group // dil
        for g in range(tm // group):
            y = jnp.dot(perm_ref[...], tb[g * group:(g + 1) * group, :],
                        preferred_element_type=F32).astype(BF16)
            for r in range(dil):
                out_ref[r, g * n:(g + 1) * n, :] = y[r * n:(r + 1) * n, :]


def _inproj_kernel(x_ref, g_ref, w_ref, qg_ref, kg_ref, cw_ref, hsum_ref, *rest):
    nd = len(DILATIONS)
    perm_refs, rest = rest[:nd], rest[nd:]
    q_refs, k_refs, v_refs = rest[:nd], rest[nd:2 * nd], rest[2 * nd:3 * nd]
    b_ref, halo_ref = rest[3 * nd:]

    @pl.when(pl.program_id(1) == 0)
    def _():
        halo_ref[...] = jnp.zeros_like(halo_ref)

    xn = _rms(x_ref[...], g_ref[...]).astype(BF16)

    def proj(idx):
        return jnp.dot(xn, w_ref[:, idx * A_WIDTH:(idx + 1) * A_WIDTH], preferred_element_type=F32)

    def head_norm(t, gain):
        t2 = (t * t).astype(BF16)
        half = A_WIDTH // 2
        ss = jnp.concatenate(
            [jnp.dot(t2[:, :half], hsum_ref[...], preferred_element_type=F32),
             jnp.dot(t2[:, half:], hsum_ref[...], preferred_element_type=F32)], axis=1)
        return t * lax.rsqrt(ss * (1.0 / A_HEAD_DIM) + EPS) * gain

    _store_streams(head_norm(proj(0), qg_ref[...]), perm_refs, q_refs)
    _store_streams(head_norm(proj(1), kg_ref[...]), perm_refs, k_refs)
    _store_streams(proj(2), perm_refs, v_refs)
    gate_b = proj(3)
    s = proj(4) * proj(5)
    conv = _causal_conv3(s, halo_ref[...], cw_ref[...])
    halo_ref[...] = s[s.shape[0] - SUBLANES:]
    b_ref[...] = (gate_b * conv).astype(BF16)


def _inproj(x, g, w_in, q_gain, k_gain, sconv_w, hsum):
    B, S, D = x.shape
    tm = ROW_TILE
    streams = [jax.ShapeDtypeStruct((B, d, S // d, A_WIDTH), BF16) for d in DILATIONS]
    stream_specs = [_stream_rows(d, tm, A_WIDTH) for d in DILATIONS]
    perms = [_stream_perm_matrix(d) for d in DILATIONS]
    outs = pl.pallas_call(
        _inproj_kernel,
        grid=(B, S // tm),
        in_specs=[_rows(tm, D), _resident((1, D)), _resident(w_in.shape), _resident((1, A_WIDTH)),
                  _resident((1, A_WIDTH)), _resident(sconv_w.shape), _resident(hsum.shape)]
                 + [_resident(p.shape) for p in perms],
        out_specs=stream_specs * 3 + [_rows(tm, B_WIDTH)],
        out_shape=streams * 3 + [jax.ShapeDtypeStruct((B, S, B_WIDTH), BF16)],
        scratch_shapes=[pltpu.VMEM((SUBLANES, B_WIDTH), F32)],
        compiler_params=_params("parallel", "arbitrary"),
        name="even_inproj",
    )(x, g, w_in, q_gain, k_gain, sconv_w, hsum, *perms)
    nd = len(DILATIONS)
    return outs[:nd], outs[nd:2 * nd], outs[2 * nd:3 * nd], outs[3 * nd]


_Q_SCALE = A_HEAD_DIM ** -0.5 * math.log2(math.e)


def _col_reduce(x, op):
    parts = x.reshape(4, x.shape[0] // 4, x.shape[1])
    part = op(op(parts[0], parts[1]), op(parts[2], parts[3]))
    reduce = jnp.max if op is jnp.maximum else jnp.sum
    return reduce(part, axis=0, keepdims=True)


def _attn_kernel(q_ref, kp_ref, kc_ref, vp_ref, vc_ref, o_ref, lse_ref,
                 k_scr, vwin_scr, bias_scr, s_scr, ot_scr, lse_scr):
    first_tile = pl.program_id(2) == 0
    hd = A_HEAD_DIM
    n_pairs = A_WIDTH // LANES
    n_streams, tq = q_ref.shape[0], q_ref.shape[1]
    blocks_per_stream = tq // REACH
    n_blocks = n_streams * blocks_per_stream
    assert blocks_per_stream & (blocks_per_stream - 1) == 0 and n_blocks % 2 == 0

    kj = lax.broadcasted_iota(jnp.int32, (2 * REACH, 2 * REACH), 0)
    qi = lax.broadcasted_iota(jnp.int32, (2 * REACH, 2 * REACH), 1) % REACH
    band = (kj >= qi) & (kj <= qi + REACH)
    bias_scr[0] = jnp.where(band & ((kj >= REACH) | jnp.logical_not(first_tile)), 0.0, -1e30)
    bias_scr[1] = jnp.where(band, 0.0, -1e30)
    for s in range(n_streams):
        v_t = jnp.concatenate([vp_ref[s], vc_ref[s]], axis=0).T
        for j in range(blocks_per_stream):
            vwin_scr[s * blocks_per_stream + j] = v_t[:, j * REACH:(j + 2) * REACH]
        for pr in range(n_pairs):
            slab = slice(pr * LANES, (pr + 1) * LANES)
            k_scr[pr, s, 0:REACH, :] = kp_ref[s, :, slab]
            k_scr[pr, s, REACH:, :] = kc_ref[s, :, slab]
    low_half = lax.broadcasted_iota(jnp.int32, (REACH, LANES), 1) < hd
    zero = jnp.zeros((), BF16)
    ones_rows = jnp.ones((2 * SUBLANES, 2 * REACH), BF16)

    def block_scores(blk, slot):
        s = lax.shift_right_logical(blk, blocks_per_stream.bit_length() - 1)
        j = blk & (blocks_per_stream - 1)
        bias = bias_scr[jnp.minimum(j, 1)]
        maxima = []
        for pr in range(n_pairs):
            k_win = k_scr[pr, s, pl.ds(pl.multiple_of(j * REACH, REACH), 2 * REACH), :]
            q_slab = q_ref[s, pl.ds(pl.multiple_of(j * REACH, REACH), REACH),
                           pr * LANES:(pr + 1) * LANES]
            q_two = jnp.concatenate([jnp.where(low_half, q_slab, zero),
                                     jnp.where(low_half, zero, q_slab)], axis=0)
            sc = lax.dot_general(k_win, q_two, (((1,), (1,)), ((), ())),
                                 preferred_element_type=F32) + bias
            s_scr[slot, pr] = sc
            maxima.append(_col_reduce(sc, jnp.maximum))
        return tuple(maxima)

    def block_finish(blk, slot, maxima):
        for pr in range(n_pairs):
            p_t = jnp.exp2(s_scr[slot, pr] - maxima[pr]).astype(BF16)
            v_pair = vwin_scr[blk, pr * LANES:(pr + 1) * LANES, :]
            o_aug = jnp.dot(jnp.concatenate([v_pair, ones_rows], axis=0), p_t,
                            preferred_element_type=F32)
            den = o_aug[2 * hd:2 * hd + 1, :]
            o_t = o_aug[:2 * hd, :] / den
            ot_scr[blk, pr * LANES:(pr + 1) * LANES, :] = jnp.concatenate(
                [o_t[:hd, :REACH], o_t[hd:, REACH:]], axis=0).astype(BF16)
            lse = (maxima[pr] + jnp.log2(den)) * math.log(2.0)
            lse_scr[blk, pr] = jnp.broadcast_to(lse, (SUBLANES, 2 * REACH))

    def two_blocks(u, maxima_even, lookahead=True):
        blk = 2 * u
        maxima_odd = block_scores(blk + 1, 1)
        block_finish(blk, 0, maxima_even)
        maxima_next = block_scores(blk + 2, 0) if lookahead else None
        block_finish(blk + 1, 1, maxima_odd)
        return maxima_next

    maxima = lax.fori_loop(0, n_blocks // 2 - 1, two_blocks, block_scores(jnp.int32(0), 0))
    two_blocks(jnp.int32(n_blocks // 2 - 1), maxima, lookahead=False)

    for blk in range(n_blocks):
        s, j = divmod(blk, blocks_per_stream)
        rows = slice(j * REACH, (j + 1) * REACH)
        o_ref[s, rows, :] = ot_scr[blk].T
        lse_rows = []
        for pr in range(n_pairs):
            lse_rows += [lse_scr[blk, pr, 0:1, :REACH], lse_scr[blk, pr, 0:1, REACH:]]
        lse_rows.append(jnp.zeros((LANES - A_HEADS, REACH), F32))
        lse_ref[s, rows, :] = jnp.concatenate(lse_rows, axis=0).T


def _dilated_attention_one(q, k, v, dil):
    B, _, L, W = q.shape
    tq = min(ATTN_ROWS, L)
    ns = min(ATTN_ROWS // tq, dil)
    n_blocks, n_pairs = ns * tq // REACH, W // LANES
    cur = pl.BlockSpec((None, ns, tq, W), lambda b, r, i: (b, r, i, 0))
    prev = pl.BlockSpec((None, ns, REACH, W),
                        lambda b, r, i: (b, r, jnp.maximum(i * (tq // REACH) - 1, 0), 0))
    return pl.pallas_call(
        _attn_kernel,
        grid=(B, dil // ns, L // tq),
        in_specs=[cur, prev, cur, prev, cur],
        out_specs=[cur, pl.BlockSpec((None, ns, tq, LANES), lambda b, r, i: (b, r, i, 0))],
        out_shape=[jax.ShapeDtypeStruct((B, dil, L, W), BF16),
                   jax.ShapeDtypeStruct((B, dil, L, LANES), F32)],
        scratch_shapes=[
            pltpu.VMEM((n_pairs, ns, REACH + tq, LANES), BF16),
            pltpu.VMEM((n_blocks, W, 2 * REACH), BF16),
            pltpu.VMEM((2, 2 * REACH, 2 * REACH), F32),
            pltpu.VMEM((2, n_pairs, 2 * REACH, 2 * REACH), F32),
            pltpu.VMEM((n_blocks, W, REACH), BF16),
            pltpu.VMEM((n_blocks, n_pairs, SUBLANES, 2 * REACH), F32),
        ],
        compiler_params=_params("parallel", "parallel", "arbitrary"),
        name=f"dilated_attn_d{dil}",
    )(q, k, k, v, v)


def _load_streams(src_ref, dil, slab_ref):
    n_slabs = src_ref.shape[-1] // LANES
    if dil == 1:
        t = src_ref[0].astype(F32)
        return [t[:, s * LANES:(s + 1) * LANES] for s in range(n_slabs)]
    n = src_ref.shape[1]
    for r in range(dil):
        for s in range(n_slabs):
            slab_ref[s, pl.ds(r, n, stride=dil), :] = (
                src_ref[r, :, s * LANES:(s + 1) * LANES].astype(F32))
    return [slab_ref[s] for s in range(n_slabs)]


def _outproj_kernel(x_ref, *rest):
    nd = len(DILATIONS)
    o_refs, l_refs = rest[:nd], rest[nd:2 * nd]
    b_ref, expand_ref, w_ref, out_ref, o_slab_ref, l_slab_ref = rest[2 * nd:]
    n_slabs = A_WIDTH // LANES

    ls = [_load_streams(l_ref, d, l_slab_ref.at[n])[0]
          for n, (d, l_ref) in enumerate(zip(DILATIONS, l_refs))]
    mx = functools.reduce(jnp.maximum, ls)
    es = [jnp.exp(l - mx) for l in ls]
    tot = functools.reduce(jnp.add, es)
    a_slabs = [None] * n_slabs
    for n, (d, e, o_ref) in enumerate(zip(DILATIONS, es, o_refs)):
        w = e / tot
        wf = jnp.dot(w.astype(BF16), expand_ref[...], preferred_element_type=F32)
        o_slabs = _load_streams(o_ref, d, o_slab_ref.at[n])
        for s in range(n_slabs):
            term = wf[:, s * LANES:(s + 1) * LANES] * o_slabs[s]
            a_slabs[s] = term if a_slabs[s] is None else a_slabs[s] + term
    a = jnp.concatenate(a_slabs, axis=1).astype(BF16)
    y = (jnp.dot(a, w_ref[:A_WIDTH, :], preferred_element_type=F32)
         + jnp.dot(b_ref[...], w_ref[A_WIDTH:, :], preferred_element_type=F32))
    out_ref[...] = x_ref[...] + y


def _outproj(x, os, lses, b, expand, w_out):
    B, S, D = x.shape
    tm = ROW_TILE
    nd = len(DILATIONS)
    return pl.pallas_call(
        _outproj_kernel,
        grid=(B, S // tm),
        in_specs=[_rows(tm, D)] + [_stream_rows(d, tm, A_WIDTH) for d in DILATIONS]
                 + [_stream_rows(d, tm, LANES) for d in DILATIONS]
                 + [_rows(tm, B_WIDTH), _resident(expand.shape), _resident(w_out.shape)],
        out_specs=_rows(tm, D),
        out_shape=jax.ShapeDtypeStruct((B, S, D), F32),
        scratch_shapes=[pltpu.VMEM((nd, A_WIDTH // LANES, tm, LANES), F32),
                        pltpu.VMEM((nd, 1, tm, LANES), F32)],
        compiler_params=_params("parallel", "parallel"),
        name="even_outproj",
    )(x, *os, *lses, b, expand, w_out)


def _ffn_kernel(x_ref, g_ref, wup_ref, cw_ref, wdn_ref, o_ref, h_ref, halo_ref):
    @pl.when(pl.program_id(1) == 0)
    def _():
        halo_ref[...] = jnp.zeros_like(halo_ref)

    x = x_ref[...]
    tm = x.shape[0]
    xn = _rms(x, g_ref[...]).astype(BF16)

    def conv_up(col):
        cols = slice(col, col + FF_TILE)
        u = jnp.dot(xn, wup_ref[:, cols], preferred_element_type=F32)
        y = _causal_conv3(u, halo_ref[:, cols], cw_ref[:, cols])
        halo_ref[:, cols] = u[tm - SUBLANES:]
        return y

    for c in range(D_FF // FF_TILE):
        gate = conv_up(c * FF_TILE)
        val = conv_up(D_FF + c * FF_TILE)
        h_ref[:, c * FF_TILE:(c + 1) * FF_TILE] = (_silu(gate) * val).astype(BF16)
    o_ref[...] = x + jnp.dot(h_ref[...], wdn_ref[...], preferred_element_type=F32)


def _ffn(x, g, w_up, conv_w, w_down):
    B, S, D = x.shape
    tm = FFN_ROW_TILE
    return pl.pallas_call(
        _ffn_kernel,
        grid=(B, S // tm),
        in_specs=[_rows(tm, D), _resident((1, D)), _resident(w_up.shape), _resident(conv_w.shape),
                  _resident(w_down.shape)],
        out_specs=_rows(tm, D),
        out_shape=jax.ShapeDtypeStruct((B, S, D), F32),
        scratch_shapes=[pltpu.VMEM((tm, D_FF), BF16), pltpu.VMEM((SUBLANES, 2 * D_FF), F32)],
        compiler_params=_params("parallel", "arbitrary"),
        name="conv_glu_ffn",
    )(x, g, w_up, conv_w, w_down)


def _rope_table_kernel(cos_ref, sin_ref):
    half = cos_ref.shape[1]
    pos = lax.broadcasted_iota(jnp.int32, cos_ref.shape, 0).astype(F32)
    j = lax.broadcasted_iota(jnp.int32, cos_ref.shape, 1).astype(F32)
    inv = jnp.exp(j * (-math.log(ROPE_BASE) / half))
    ang = pos * inv
    cos_ref[...] = jnp.cos(ang)
    sin_ref[...] = jnp.sin(ang)


def _rope_table(S):
    half = RET_KDIM // 2
    shape = jax.ShapeDtypeStruct((S, half), F32)
    return pl.pallas_call(_rope_table_kernel, out_shape=[shape, shape], name="rope_table")()


def _retproj_kernel(x_ref, g_ref, wq_ref, wk_ref, wg_ref, cos_ref, sin_ref,
                    q_ref, k_ref, xn_ref, gate_ref):
    xn = _rms(x_ref[...], g_ref[...]).astype(BF16)
    xn_ref[...] = xn
    cos = cos_ref[...]
    sin = sin_ref[...]
    half = RET_KDIM // 2

    def rotary_store(t, dst_ref):
        for h in range(RET_HEADS):
            x1 = t[:, h * RET_KDIM:h * RET_KDIM + half]
            x2 = t[:, h * RET_KDIM + half:(h + 1) * RET_KDIM]
            dst_ref[:, h * RET_KDIM:h * RET_KDIM + half] = (x1 * cos - x2 * sin).astype(BF16)
            dst_ref[:, h * RET_KDIM + half:(h + 1) * RET_KDIM] = (x1 * sin + x2 * cos).astype(BF16)

    rotary_store(jnp.dot(xn, wq_ref[...], preferred_element_type=F32), q_ref)
    rotary_store(jnp.dot(xn, wk_ref[...], preferred_element_type=F32) * (RET_KDIM ** -0.5), k_ref)
    gate_ref[...] = _silu(jnp.dot(xn, wg_ref[...], preferred_element_type=F32)).astype(BF16)


def _retproj(x, g, wq, wk, wg, cos, sin):
    B, S, D = x.shape
    tm = ROW_TILE
    kw, vw = RET_HEADS * RET_KDIM, RET_HEADS * RET_VDIM
    table = pl.BlockSpec((tm, RET_KDIM // 2), lambda b, i: (i, 0))
    return pl.pallas_call(
        _retproj_kernel,
        grid=(B, S // tm),
        in_specs=[_rows(tm, D), _resident((1, D)), _resident(wq.shape), _resident(wk.shape),
                  _resident(wg.shape), table, table],
        out_specs=[_rows(tm, kw), _rows(tm, kw), _rows(tm, D), _rows(tm, vw)],
        out_shape=[jax.ShapeDtypeStruct((B, S, kw), BF16), jax.ShapeDtypeStruct((B, S, kw), BF16),
                   jax.ShapeDtypeStruct((B, S, D), BF16), jax.ShapeDtypeStruct((B, S, vw), BF16)],
        compiler_params=_params("parallel", "parallel"),
        name="ret_proj",
    )(x, g, wq, wk, wg, cos, sin)


def _retention_kernel(x_ref, q_ref, k_ref, xn_ref, gate_ref, wv_ref, wo_ref, o_ref, state_ref,
                      decay_ref, a_ref, v_ref):
    C = RET_CHUNK
    log_gs = [math.log1p(-(2.0 ** (-5.0 - h))) for h in range(RET_HEADS)]

    @pl.when((pl.program_id(0) == 0) & (pl.program_id(1) == 0))
    def _():
        ii = lax.broadcasted_iota(jnp.int32, (C, C), 0)
        jj = lax.broadcasted_iota(jnp.int32, (C, C), 1)
        diff = (ii - jj).astype(F32)
        for h in range(RET_HEADS):
            decay_ref[h] = jnp.where(diff >= 0, jnp.exp(log_gs[h] * jnp.maximum(diff, 0.0)), 0.0)

    @pl.when(pl.program_id(1) == 0)
    def _():
        state_ref[...] = jnp.zeros_like(state_ref)

    pos = lax.broadcasted_iota(jnp.int32, (C, 1), 0).astype(F32)
    q_decays = [jnp.exp(lg * (pos + 1.0)) for lg in log_gs]
    k_decays = [jnp.exp(lg * (C - 1.0 - pos)) for lg in log_gs]

    for c in range(q_ref.shape[0] // C):
        rows = slice(c * C, (c + 1) * C)
        v_ref[rows, :] = jnp.dot(xn_ref[rows, :], wv_ref[...],
                                 preferred_element_type=F32).astype(BF16)
        for h in range(RET_HEADS):
            kcols = slice(h * RET_KDIM, (h + 1) * RET_KDIM)
            vcols = slice(h * RET_VDIM, (h + 1) * RET_VDIM)
            qc = q_ref[rows, kcols]
            kc = k_ref[rows, kcols]
            vc = v_ref[rows, vcols]
            state = state_ref[h]
            scores = lax.dot_general(qc, kc, (((1,), (1,)), ((), ())),
                                     preferred_element_type=F32) * decay_ref[h]
            o = (jnp.dot(scores.astype(BF16), vc, preferred_element_type=F32)
                 + jnp.dot(qc, state.astype(BF16), preferred_element_type=F32) * q_decays[h])
            kd = (kc.astype(F32) * k_decays[h]).astype(BF16)
            state_ref[h] = state * math.exp(log_gs[h] * C) + lax.dot_general(
                kd, vc, (((0,), (0,)), ((), ())), preferred_element_type=F32)
            mu = jnp.mean(o, axis=-1, keepdims=True)
            var = jnp.mean(jnp.square(o - mu), axis=-1, keepdims=True)
            on = (o - mu) * lax.rsqrt(var + EPS)
            a_ref[rows, vcols] = (gate_ref[rows, vcols].astype(F32) * on).astype(BF16)
        o_ref[rows, :] = x_ref[rows, :] + jnp.dot(a_ref[rows, :], wo_ref[...],
                                                  preferred_element_type=F32)


def _retention(x, q, k, xn, gate, wv, wo):
    B, S, D = x.shape
    kw, vw = q.shape[-1], gate.shape[-1]
    tc = RET_TILE
    return pl.pallas_call(
        _retention_kernel,
        grid=(B, S // tc),
        in_specs=[_rows(tc, D), _rows(tc, kw), _rows(tc, kw), _rows(tc, D), _rows(tc, vw),
                  _resident(wv.shape), _resident(wo.shape)],
        out_specs=_rows(tc, D),
        out_shape=jax.ShapeDtypeStruct((B, S, D), F32),
        scratch_shapes=[pltpu.VMEM((RET_HEADS, RET_KDIM, RET_VDIM), F32),
                        pltpu.VMEM((RET_HEADS, RET_CHUNK, RET_CHUNK), F32),
                        pltpu.VMEM((tc, vw), BF16), pltpu.VMEM((tc, vw), BF16)],
        compiler_params=_params("arbitrary", "arbitrary"),
        name="retention",
    )(x, q, k, xn, gate, wv, wo)


def _head_sum_matrix():
    g = jnp.arange(A_WIDTH // 2) // A_HEAD_DIM
    return (g[:, None] == g[None, :]).astype(BF16)


def _head_expand_matrix():
    head_of_lane = jnp.arange(A_WIDTH) // A_HEAD_DIM
    return (jnp.arange(LANES)[:, None] == head_of_lane[None, :]).astype(BF16)


def kernel(x, even_norm, even_w_in, even_q_gain, even_k_gain, even_sconv_w, even_w_out, odd_norm,
           ret_wq, ret_wk, ret_wv, ret_wg, ret_gn_gain, ret_wo, ffn_norm, ffn_w_up, ffn_conv_w,
           ffn_w_down):
    B, S, D = x.shape
    depth = ffn_norm.shape[0]
    bf = lambda w: w.astype(BF16)
    row = lambda v: v.reshape(1, -1).astype(F32)
    hsum = _head_sum_matrix()
    expand = _head_expand_matrix()
    cos = sin = None

    for l in range(depth):
        if l % 2 == 0:
            e = l // 2
            qs, ks, vs, b = _inproj(x, row(even_norm[e]), bf(even_w_in[e]),
                                    row(jnp.tile(even_q_gain[e], A_HEADS) * _Q_SCALE),
                                    row(jnp.tile(even_k_gain[e], A_HEADS)),
                                    even_sconv_w[e], hsum)
            os, lses = zip(*[_dilated_attention_one(q, k, v, d)
                             for q, k, v, d in zip(qs, ks, vs, DILATIONS)])
            x = _outproj(x, os, lses, b, expand, bf(even_w_out[e]))
        else:
            o = l // 2
            if cos is None:
                cos, sin = _rope_table(S)
            q, k, xn, gate = _retproj(x, row(odd_norm[o]), bf(ret_wq[o]), bf(ret_wk[o]),
                                      bf(ret_wg[o]), cos, sin)
            x = _retention(x, q, k, xn, gate, bf(ret_wv[o]),
                           bf(ret_gn_gain[o].reshape(-1, 1) * ret_wo[o]))
        x = _ffn(x, row(ffn_norm[l]), bf(ffn_w_up[l]), ffn_conv_w[l], bf(ffn_w_down[l]))
    return x
```

```python
import functools
import math

import jax
import jax.numpy as jnp
from jax import lax
from jax.experimental import pallas as pl
from jax.experimental.pallas import tpu as pltpu

F32 = jnp.float32
BF16 = jnp.bfloat16

D_MODEL = 1024
A_HEADS = 8
A_HEAD_DIM = 64
A_WIDTH = A_HEADS * A_HEAD_DIM
B_WIDTH = D_MODEL - A_WIDTH
DILATIONS = (1, 4, 16)
REACH = 128
RET_HEADS = 4
RET_KDIM = D_MODEL // RET_HEADS
RET_VDIM = 2 * D_MODEL // RET_HEADS
RET_CHUNK = 256
ROPE_BASE = 10000.0
D_FF = 2816
EPS = 1e-6

SUBLANES = 8
LANES = 128
VMEM_LIMIT_BYTES = 56 * 1024 * 1024

ROW_TILE = 1024
FFN_ROW_TILE = 1024
FF_TILE = 256
ATTN_ROWS = 2048
RET_TILE = 512


def _params(*semantics):
    return pltpu.CompilerParams(dimension_semantics=semantics, vmem_limit_bytes=VMEM_LIMIT_BYTES)


def _resident(shape):
    nd = len(shape)
    return pl.BlockSpec(shape, lambda *_: (0,) * nd, pipeline_mode=pl.Buffered(1))


def _rows(tm, width):
    return pl.BlockSpec((None, tm, width), lambda b, i: (b, i, 0))


def _stream_rows(dil, tm, width):
    return pl.BlockSpec((None, dil, tm // dil, width), lambda b, i: (b, 0, i, 0))


def _rms(x, g):
    ms = jnp.mean(x * x, axis=-1, keepdims=True)
    return x * lax.rsqrt(ms + EPS) * g


def _silu(g):
    return g / (1.0 + jnp.exp2(g * -math.log2(math.e)))


def _shift_rows(cur, prev, k):
    rolled = pltpu.roll(cur, k, axis=0)
    prev_rolled = pltpu.roll(prev, k, axis=0)
    row = lax.broadcasted_iota(jnp.int32, prev.shape, 0)
    head = jnp.where(row < k, prev_rolled, rolled[:SUBLANES])
    return jnp.concatenate([head, rolled[SUBLANES:]], axis=0)


def _causal_conv3(cur, prev, w):
    return (_shift_rows(cur, prev, 2) * w[0:1] + _shift_rows(cur, prev, 1) * w[1:2]
            + cur * w[2:3])


def _stream_group(dil):
    return max(LANES, 2 * SUBLANES * dil)


def _stream_perm_matrix(dil):
    group = _stream_group(dil)
    dst = jnp.arange(group)
    src = (dst % (group // dil)) * dil + dst // (group // dil)
    return (jnp.arange(group)[None, :] == src[:, None]).astype(BF16)


def _store_streams(t, perm_refs, out_refs):
    tm = t.shape[0]
    tb = t.astype(BF16)
    for dil, perm_ref, out_ref in zip(DILATIONS, perm_refs, out_refs):
        if dil == 1:
            out_ref[0] = tb
            continue
        group = perm_ref.shape[0]
        n = group // dil
        for g in range(tm // group):
            y = jnp.dot(perm_ref[...], tb[g * group:(g + 1) * group, :],
                        preferred_element_type=F32).astype(BF16)
            for r in range(dil):
                out_ref[r, g * n:(g + 1) * n, :] = y[r * n:(r + 1) * n, :]


def _inproj_kernel(x_ref, g_ref, w_ref, qg_ref, kg_ref, hsum_ref, *rest):
    nd = len(DILATIONS)
    perm_refs, rest = rest[:nd], rest[nd:]
    q_refs, k_refs, v_refs = rest[:nd], rest[nd:2 * nd], rest[2 * nd:3 * nd]

    xn = _rms(x_ref[...], g_ref[...]).astype(BF16)

    def proj(idx):
        return jnp.dot(xn, w_ref[:, idx * A_WIDTH:(idx + 1) * A_WIDTH], preferred_element_type=F32)

    def head_norm(t, gain):
        t2 = (t * t).astype(BF16)
        half = A_WIDTH // 2
        ss = jnp.concatenate(
            [jnp.dot(t2[:, :half], hsum_ref[...], preferred_element_type=F32),
             jnp.dot(t2[:, half:], hsum_ref[...], preferred_element_type=F32)], axis=1)
        return t * lax.rsqrt(ss * (1.0 / A_HEAD_DIM) + EPS) * gain

    _store_streams(head_norm(proj(0), qg_ref[...]), perm_refs, q_refs)
    _store_streams(head_norm(proj(1), kg_ref[...]), perm_refs, k_refs)
    _store_streams(proj(2), perm_refs, v_refs)


def _inproj(x, g, w_qkv, q_gain, k_gain, hsum):
    B, S, D = x.shape
    tm = ROW_TILE
    streams = [jax.ShapeDtypeStruct((B, d, S // d, A_WIDTH), BF16) for d in DILATIONS]
    stream_specs = [_stream_rows(d, tm, A_WIDTH) for d in DILATIONS]
    perms = [_stream_perm_matrix(d) for d in DILATIONS]
    outs = pl.pallas_call(
        _inproj_kernel,
        grid=(B, S // tm),
        in_specs=[_rows(tm, D), _resident((1, D)), _resident(w_qkv.shape), _resident((1, A_WIDTH)),
                  _resident((1, A_WIDTH)), _resident(hsum.shape)]
                 + [_resident(p.shape) for p in perms],
        out_specs=stream_specs * 3,
        out_shape=streams * 3,
        compiler_params=_params("parallel", "parallel"),
        name="even_inproj",
    )(x, g, w_qkv, q_gain, k_gain, hsum, *perms)
    nd = len(DILATIONS)
    return outs[:nd], outs[nd:2 * nd], outs[2 * nd:3 * nd]


_Q_SCALE = A_HEAD_DIM ** -0.5 * math.log2(math.e)


def _col_reduce(x, op):
    parts = x.reshape(4, x.shape[0] // 4, x.shape[1])
    part = op(op(parts[0], parts[1]), op(parts[2], parts[3]))
    reduce = jnp.max if op is jnp.maximum else jnp.sum
    return reduce(part, axis=0, keepdims=True)


def _attn_kernel(q_ref, kp_ref, kc_ref, vp_ref, vc_ref, o_ref, lse_ref,
                 k_scr, vwin_scr, bias_scr, s_scr, ot_scr, lse_scr):
    first_tile = pl.program_id(2) == 0
    hd = A_HEAD_DIM
    n_pairs = A_WIDTH // LANES
    n_streams, tq = q_ref.shape[0], q_ref.shape[1]
    blocks_per_stream = tq // REACH
    n_blocks = n_streams * blocks_per_stream
    assert blocks_per_stream & (blocks_per_stream - 1) == 0 and n_blocks % 2 == 0

    kj = lax.broadcasted_iota(jnp.int32, (2 * REACH, 2 * REACH), 0)
    qi = lax.broadcasted_iota(jnp.int32, (2 * REACH, 2 * REACH), 1) % REACH
    band = (kj >= qi) & (kj <= qi + REACH)
    bias_scr[0] = jnp.where(band & ((kj >= REACH) | jnp.logical_not(first_tile)), 0.0, -1e30)
    bias_scr[1] = jnp.where(band, 0.0, -1e30)
    for s in range(n_streams):
        v_t = jnp.concatenate([vp_ref[s], vc_ref[s]], axis=0).T
        for j in range(blocks_per_stream):
            vwin_scr[s * blocks_per_stream + j] = v_t[:, j * REACH:(j + 2) * REACH]
        for pr in range(n_pairs):
            slab = slice(pr * LANES, (pr + 1) * LANES)
            k_scr[pr, s, 0:REACH, :] = kp_ref[s, :, slab]
            k_scr[pr, s, REACH:, :] = kc_ref[s, :, slab]
    low_half = lax.broadcasted_iota(jnp.int32, (REACH, LANES), 1) < hd
    zero = jnp.zeros((), BF16)
    ones_rows = jnp.ones((2 * SUBLANES, 2 * REACH), BF16)

    def block_scores(blk, slot):
        s = lax.shift_right_logical(blk, blocks_per_stream.bit_length() - 1)
        j = blk & (blocks_per_stream - 1)
        bias = bias_scr[jnp.minimum(j, 1)]
        maxima = []
        for pr in range(n_pairs):
            k_win = k_scr[pr, s, pl.ds(pl.multiple_of(j * REACH, REACH), 2 * REACH), :]
            q_slab = q_ref[s, pl.ds(pl.multiple_of(j * REACH, REACH), REACH),
                           pr * LANES:(pr + 1) * LANES]
            q_two = jnp.concatenate([jnp.where(low_half, q_slab, zero),
                                     jnp.where(low_half, zero, q_slab)], axis=0)
            sc = lax.dot_general(k_win, q_two, (((1,), (1,)), ((), ())),
                                 preferred_element_type=F32) + bias
            s_scr[slot, pr] = sc
            maxima.append(_col_reduce(sc, jnp.maximum))
        return tuple(maxima)

    def block_finish(blk, slot, maxima):
        for pr in range(n_pairs):
            p_t = jnp.exp2(s_scr[slot, pr] - maxima[pr]).astype(BF16)
            v_pair = vwin_scr[blk, pr * LANES:(pr + 1) * LANES, :]
            o_aug = jnp.dot(jnp.concatenate([v_pair, ones_rows], axis=0), p_t,
                            preferred_element_type=F32)
            den = o_aug[2 * hd:2 * hd + 1, :]
            o_t = o_aug[:2 * hd, :] / den
            ot_scr[blk, pr * LANES:(pr + 1) * LANES, :] = jnp.concatenate(
                [o_t[:hd, :REACH], o_t[hd:, REACH:]], axis=0).astype(BF16)
            lse = (maxima[pr] + jnp.log2(den)) * math.log(2.0)
            lse_scr[blk, pr] = jnp.broadcast_to(lse, (SUBLANES, 2 * REACH))

    def two_blocks(u, maxima_even, lookahead=True):
        blk = 2 * u
        maxima_odd = block_scores(blk + 1, 1)
        block_finish(blk, 0, maxima_even)
        maxima_next = block_scores(blk + 2, 0) if lookahead else None
        block_finish(blk + 1, 1, maxima_odd)
        return maxima_next

    maxima = lax.fori_loop(0, n_blocks // 2 - 1, two_blocks, block_scores(jnp.int32(0), 0))
    two_blocks(jnp.int32(n_blocks // 2 - 1), maxima, lookahead=False)

    for blk in range(n_blocks):
        s, j = divmod(blk, blocks_per_stream)
        rows = slice(j * REACH, (j + 1) * REACH)
        o_ref[s, rows, :] = ot_scr[blk].T
        lse_rows = []
        for pr in range(n_pairs):
            lse_rows += [lse_scr[blk, pr, 0:1, :REACH], lse_scr[blk, pr, 0:1, REACH:]]
        lse_rows.append(jnp.zeros((LANES - A_HEADS, REACH), F32))
        lse_ref[s, rows, :] = jnp.concatenate(lse_rows, axis=0).T


def _dilated_attention_one(q, k, v, dil):
    B, _, L, W = q.shape
    tq = min(ATTN_ROWS, L)
    ns = min(ATTN_ROWS // tq, dil)
    n_blocks, n_pairs = ns * tq // REACH, W // LANES
    cur = pl.BlockSpec((None, ns, tq, W), lambda b, r, i: (b, r, i, 0))
    prev = pl.BlockSpec((None, ns, REACH, W),
                        lambda b, r, i: (b, r, jnp.maximum(i * (tq // REACH) - 1, 0), 0))
    return pl.pallas_call(
        _attn_kernel,
        grid=(B, dil // ns, L // tq),
        in_specs=[cur, prev, cur, prev, cur],
        out_specs=[cur, pl.BlockSpec((None, ns, tq, LANES), lambda b, r, i: (b, r, i, 0))],
        out_shape=[jax.ShapeDtypeStruct((B, dil, L, W), BF16),
                   jax.ShapeDtypeStruct((B, dil, L, LANES), F32)],
        scratch_shapes=[
            pltpu.VMEM((n_pairs, ns, REACH + tq, LANES), BF16),
            pltpu.VMEM((n_blocks, W, 2 * REACH), BF16),
            pltpu.VMEM((2, 2 * REACH, 2 * REACH), F32),
            pltpu.VMEM((2, n_pairs, 2 * REACH, 2 * REACH), F32),
            pltpu.VMEM((n_blocks, W, REACH), BF16),
            pltpu.VMEM((n_blocks, n_pairs, SUBLANES, 2 * REACH), F32),
        ],
        compiler_params=_params("parallel", "parallel", "arbitrary"),
        name=f"dilated_attn_d{dil}",
    )(q, k, k, v, v)


def _load_streams(src_ref, dil, slab_ref):
    n_slabs = src_ref.shape[-1] // LANES
    if dil == 1:
        t = src_ref[0].astype(F32)
        return [t[:, s * LANES:(s + 1) * LANES] for s in range(n_slabs)]
    n = src_ref.shape[1]
    for r in range(dil):
        for s in range(n_slabs):
            slab_ref[s, pl.ds(r, n, stride=dil), :] = (
                src_ref[r, :, s * LANES:(s + 1) * LANES].astype(F32))
    return [slab_ref[s] for s in range(n_slabs)]


def _outproj_kernel(x_ref, *rest):
    nd = len(DILATIONS)
    o_refs, l_refs = rest[:nd], rest[nd:2 * nd]
    (g_ref, wb_ref, cw_ref, expand_ref, w_ref, out_ref,
     o_slab_ref, l_slab_ref, halo_ref) = rest[2 * nd:]
    n_slabs = A_WIDTH // LANES

    @pl.when(pl.program_id(1) == 0)
    def _():
        halo_ref[...] = jnp.zeros_like(halo_ref)

    x = x_ref[...]
    xn = _rms(x, g_ref[...]).astype(BF16)

    def proj(idx):
        return jnp.dot(xn, wb_ref[:, idx * B_WIDTH:(idx + 1) * B_WIDTH], preferred_element_type=F32)

    gate_b = proj(0)
    s = proj(1) * proj(2)
    conv = _causal_conv3(s, halo_ref[...], cw_ref[...])
    halo_ref[...] = s[s.shape[0] - SUBLANES:]
    b = (gate_b * conv).astype(BF16)

    ls = [_load_streams(l_ref, d, l_slab_ref.at[n])[0]
          for n, (d, l_ref) in enumerate(zip(DILATIONS, l_refs))]
    mx = functools.reduce(jnp.maximum, ls)
    es = [jnp.exp(l - mx) for l in ls]
    tot = functools.reduce(jnp.add, es)
    a_slabs = [None] * n_slabs
    for n, (d, e, o_ref) in enumerate(zip(DILATIONS, es, o_refs)):
        w = e / tot
        wf = jnp.dot(w.astype(BF16), expand_ref[...], preferred_element_type=F32)
        o_slabs = _load_streams(o_ref, d, o_slab_ref.at[n])
        for s in range(n_slabs):
            term = wf[:, s * LANES:(s + 1) * LANES] * o_slabs[s]
            a_slabs[s] = term if a_slabs[s] is None else a_slabs[s] + term
    a = jnp.concatenate(a_slabs, axis=1).astype(BF16)
    y = (jnp.dot(a, w_ref[:A_WIDTH, :], preferred_element_type=F32)
         + jnp.dot(b, w_ref[A_WIDTH:, :], preferred_element_type=F32))
    out_ref[...] = x + y


def _outproj(x, os, lses, g, w_bch, sconv_w, expand, w_out):
    B, S, D = x.shape
    tm = ROW_TILE
    nd = len(DILATIONS)
    return pl.pallas_call(
        _outproj_kernel,
        grid=(B, S // tm),
        in_specs=[_rows(tm, D)] + [_stream_rows(d, tm, A_WIDTH) for d in DILATIONS]
                 + [_stream_rows(d, tm, LANES) for d in DILATIONS]
                 + [_resident((1, D)), _resident(w_bch.shape), _resident(sconv_w.shape),
                    _resident(expand.shape), _resident(w_out.shape)],
        out_specs=_rows(tm, D),
        out_shape=jax.ShapeDtypeStruct((B, S, D), F32),
        scratch_shapes=[pltpu.VMEM((nd, A_WIDTH // LANES, tm, LANES), F32),
                        pltpu.VMEM((nd, 1, tm, LANES), F32),
                        pltpu.VMEM((SUBLANES, B_WIDTH), F32)],
        compiler_params=_params("parallel", "arbitrary"),
        name="even_outproj",
    )(x, *os, *lses, g, w_bch, sconv_w, expand, w_out)


def _ffn_kernel(x_ref, g_ref, wup_ref, cw_ref, wdn_ref, o_ref, h_ref, halo_ref):
    @pl.when(pl.program_id(1) == 0)
    def _():
        halo_ref[...] = jnp.zeros_like(halo_ref)

    x = x_ref[...]
    tm = x.shape[0]
    xn = _rms(x, g_ref[...]).astype(BF16)

    def conv_up(col):
        cols = slice(col, col + FF_TILE)
        u = jnp.dot(xn, wup_ref[:, cols], preferred_element_type=F32)
        y = _causal_conv3(u, halo_ref[:, cols], cw_ref[:, cols])
        halo_ref[:, cols] = u[tm - SUBLANES:]
        return y

    for c in range(D_FF // FF_TILE):
        gate = conv_up(c * FF_TILE)
        val = conv_up(D_FF + c * FF_TILE)
        h_ref[:, c * FF_TILE:(c + 1) * FF_TILE] = (_silu(gate) * val).astype(BF16)
    o_ref[...] = x + jnp.dot(h_ref[...], wdn_ref[...], preferred_element_type=F32)


def _ffn(x, g, w_up, conv_w, w_down):
    B, S, D = x.shape
    tm = FFN_ROW_TILE
    return pl.pallas_call(
        _ffn_kernel,
        grid=(B, S // tm),
        in_specs=[_rows(tm, D), _resident((1, D)), _resident(w_up.shape), _resident(conv_w.shape),
                  _resident(w_down.shape)],
        out_specs=_rows(tm, D),
        out_shape=jax.ShapeDtypeStruct((B, S, D), F32),
        scratch_shapes=[pltpu.VMEM((tm, D_FF), BF16), pltpu.VMEM((SUBLANES, 2 * D_FF), F32)],
        compiler_params=_params("parallel", "arbitrary"),
        name="conv_glu_ffn",
    )(x, g, w_up, conv_w, w_down)


def _rope_table_kernel(cos_ref, sin_ref):
    half = cos_ref.shape[1]
    pos = lax.broadcasted_iota(jnp.int32, cos_ref.shape, 0).astype(F32)
    j = lax.broadcasted_iota(jnp.int32, cos_ref.shape, 1).astype(F32)
    inv = jnp.exp(j * (-math.log(ROPE_BASE) / half))
    ang = pos * inv
    cos_ref[...] = jnp.cos(ang)
    sin_ref[...] = jnp.sin(ang)


def _rope_table(S):
    half = RET_KDIM // 2
    shape = jax.ShapeDtypeStruct((S, half), F32)
    return pl.pallas_call(_rope_table_kernel, out_shape=[shape, shape], name="rope_table")()


def _retproj_kernel(x_ref, g_ref, wq_ref, wk_ref, wg_ref, cos_ref, sin_ref,
                    q_ref, k_ref, xn_ref, gate_ref):
    xn = _rms(x_ref[...], g_ref[...]).astype(BF16)
    xn_ref[...] = xn
    cos = cos_ref[...]
    sin = sin_ref[...]
    half = RET_KDIM // 2

    def rotary_store(t, dst_ref):
        for h in range(RET_HEADS):
            x1 = t[:, h * RET_KDIM:h * RET_KDIM + half]
            x2 = t[:, h * RET_KDIM + half:(h + 1) * RET_KDIM]
            dst_ref[:, h * RET_KDIM:h * RET_KDIM + half] = (x1 * cos - x2 * sin).astype(BF16)
            dst_ref[:, h * RET_KDIM + half:(h + 1) * RET_KDIM] = (x1 * sin + x2 * cos).astype(BF16)

    rotary_store(jnp.dot(xn, wq_ref[...], preferred_element_type=F32), q_ref)
    rotary_store(jnp.dot(xn, wk_ref[...], preferred_element_type=F32) * (RET_KDIM ** -0.5), k_ref)
    gate_ref[...] = _silu(jnp.dot(xn, wg_ref[...], preferred_element_type=F32)).astype(BF16)


def _retproj(x, g, wq, wk, wg, cos, sin):
    B, S, D = x.shape
    tm = ROW_TILE
    kw, vw = RET_HEADS * RET_KDIM, RET_HEADS * RET_VDIM
    table = pl.BlockSpec((tm, RET_KDIM // 2), lambda b, i: (i, 0))
    return pl.pallas_call(
        _retproj_kernel,
        grid=(B, S // tm),
        in_specs=[_rows(tm, D), _resident((1, D)), _resident(wq.shape), _resident(wk.shape),
                  _resident(wg.shape), table, table],
        out_specs=[_rows(tm, kw), _rows(tm, kw), _rows(tm, D), _rows(tm, vw)],
        out_shape=[jax.ShapeDtypeStruct((B, S, kw), BF16), jax.ShapeDtypeStruct((B, S, kw), BF16),
                   jax.ShapeDtypeStruct((B, S, D), BF16), jax.ShapeDtypeStruct((B, S, vw), BF16)],
        compiler_params=_params("parallel", "parallel"),
        name="ret_proj",
    )(x, g, wq, wk, wg, cos, sin)


def _retention_kernel(x_ref, q_ref, k_ref, xn_ref, gate_ref, wv_ref, wo_ref, o_ref, state_ref,
                      decay_ref, a_ref, v_ref):
    C = RET_CHUNK
    log_gs = [math.log1p(-(2.0 ** (-5.0 - h))) for h in range(RET_HEADS)]

    @pl.when((pl.program_id(0) == 0) & (pl.program_id(1) == 0))
    def _():
        ii = lax.broadcasted_iota(jnp.int32, (C, C), 0)
        jj = lax.broadcasted_iota(jnp.int32, (C, C), 1)
        diff = (ii - jj).astype(F32)
        for h in range(RET_HEADS):
            decay_ref[h] = jnp.where(diff >= 0, jnp.exp(log_gs[h] * jnp.maximum(diff, 0.0)), 0.0)

    @pl.when(pl.program_id(1) == 0)
    def _():
        state_ref[...] = jnp.zeros_like(state_ref)

    pos = lax.broadcasted_iota(jnp.int32, (C, 1), 0).astype(F32)
    q_decays = [jnp.exp(lg * (pos + 1.0)) for lg in log_gs]
    k_decays = [jnp.exp(lg * (C - 1.0 - pos)) for lg in log_gs]

    for c in range(q_ref.shape[0] // C):
        rows = slice(c * C, (c + 1) * C)
        v_ref[rows, :] = jnp.dot(xn_ref[rows, :], wv_ref[...],
                                 preferred_element_type=F32).astype(BF16)
        for h in range(RET_HEADS):
            kcols = slice(h * RET_KDIM, (h + 1) * RET_KDIM)
            vcols = slice(h * RET_VDIM, (h + 1) * RET_VDIM)
            qc = q_ref[rows, kcols]
            kc = k_ref[rows, kcols]
            vc = v_ref[rows, vcols]
            state = state_ref[h]
            scores = lax.dot_general(qc, kc, (((1,), (1,)), ((), ())),
                                     preferred_element_type=F32) * decay_ref[h]
            o = (jnp.dot(scores.astype(BF16), vc, preferred_element_type=F32)
                 + jnp.dot(qc, state.astype(BF16), preferred_element_type=F32) * q_decays[h])
            kd = (kc.astype(F32) * k_decays[h]).astype(BF16)
            state_ref[h] = state * math.exp(log_gs[h] * C) + lax.dot_general(
                kd, vc, (((0,), (0,)), ((), ())), preferred_element_type=F32)
            mu = jnp.mean(o, axis=-1, keepdims=True)
            var = jnp.mean(jnp.square(o - mu), axis=-1, keepdims=True)
            on = (o - mu) * lax.rsqrt(var + EPS)
            a_ref[rows, vcols] = (gate_ref[rows, vcols].astype(F32) * on).astype(BF16)
        o_ref[rows, :] = x_ref[rows, :] + jnp.dot(a_ref[rows, :], wo_ref[...],
                                                  preferred_element_type=F32)


def _retention(x, q, k, xn, gate, wv, wo):
    B, S, D = x.shape
    kw, vw = q.shape[-1], gate.shape[-1]
    tc = RET_TILE
    return pl.pallas_call(
        _retention_kernel,
        grid=(B, S // tc),
        in_specs=[_rows(tc, D), _rows(tc, kw), _rows(tc, kw), _rows(tc, D), _rows(tc, vw),
                  _resident(wv.shape), _resident(wo.shape)],
        out_specs=_rows(tc, D),
        out_shape=jax.ShapeDtypeStruct((B, S, D), F32),
        scratch_shapes=[pltpu.VMEM((RET_HEADS, RET_KDIM, RET_VDIM), F32),
                        pltpu.VMEM((RET_HEADS, RET_CHUNK, RET_CHUNK), F32),
                        pltpu.VMEM((tc, vw), BF16), pltpu.VMEM((tc, vw), BF16)],
        compiler_params=_params("arbitrary", "arbitrary"),
        name="retention",
    )(x, q, k, xn, gate, wv, wo)


def _head_sum_matrix():
    g = jnp.arange(A_WIDTH // 2) // A_HEAD_DIM
    return (g[:, None] == g[None, :]).astype(BF16)


def _head_expand_matrix():
    head_of_lane = jnp.arange(A_WIDTH) // A_HEAD_DIM
    return (jnp.arange(LANES)[:, None] == head_of_lane[None, :]).astype(BF16)


def kernel(x, even_norm, even_w_in, even_q_gain, even_k_gain, even_sconv_w, even_w_out, odd_norm,
           ret_wq, ret_wk, ret_wv, ret_wg, ret_gn_gain, ret_wo, ffn_norm, ffn_w_up, ffn_conv_w,
           ffn_w_down):
    B, S, D = x.shape
    depth = ffn_norm.shape[0]
    bf = lambda w: w.astype(BF16)
    row = lambda v: v.reshape(1, -1).astype(F32)
    hsum = _head_sum_matrix()
    expand = _head_expand_matrix()
    cos = sin = None

    for l in range(depth):
        if l % 2 == 0:
            e = l // 2
            n_qkv = 3 * A_WIDTH
            qs, ks, vs = _inproj(x, row(even_norm[e]), bf(even_w_in[e][:, :n_qkv]),
                                 row(jnp.tile(even_q_gain[e], A_HEADS) * _Q_SCALE),
                                 row(jnp.tile(even_k_gain[e], A_HEADS)), hsum)
            os, lses = zip(*[_dilated_attention_one(q, k, v, d)
                             for q, k, v, d in zip(qs, ks, vs, DILATIONS)])
            x = _outproj(x, os, lses, row(even_norm[e]), bf(even_w_in[e][:, n_qkv:]),
                         even_sconv_w[e], expand, bf(even_w_out[e]))
        else:
            o = l // 2
            if cos is None:
                cos, sin = _rope_table(S)
            q, k, xn, gate = _retproj(x, row(odd_norm[o]), bf(ret_wq[o]), bf(ret_wk[o]),
                                      bf(ret_wg[o]), cos, sin)
            x = _retention(x, q, k, xn, gate, bf(ret_wv[o]),
                           bf(ret_gn_gain[o].reshape(-1, 1) * ret_wo[o]))
        x = _ffn(x, row(ffn_norm[l]), bf(ffn_w_up[l]), ffn_conv_w[l], bf(ffn_w_down[l]))
    return x
```

```python
import functools
import math

import jax
import jax.numpy as jnp
from jax import lax
from jax.experimental import pallas as pl
from jax.experimental.pallas import tpu as pltpu

F32 = jnp.float32
BF16 = jnp.bfloat16

D_MODEL = 1024
A_HEADS = 8
A_HEAD_DIM = 64
A_WIDTH = A_HEADS * A_HEAD_DIM
B_WIDTH = D_MODEL - A_WIDTH
DILATIONS = (1, 4, 16)
REACH = 128
RET_HEADS = 4
RET_KDIM = D_MODEL // RET_HEADS
RET_VDIM = 2 * D_MODEL // RET_HEADS
RET_CHUNK = 256
ROPE_BASE = 10000.0
D_FF = 2816
EPS = 1e-6

SUBLANES = 8
LANES = 128
VMEM_LIMIT_BYTES = 56 * 1024 * 1024

ROW_TILE = 1024
FFN_ROW_TILE = 1024
FF_TILE = 256
ATTN_ROWS = 2048
RET_TILE = 512


def _params(*semantics):
    return pltpu.CompilerParams(dimension_semantics=semantics, vmem_limit_bytes=VMEM_LIMIT_BYTES)


def _resident(shape):
    nd = len(shape)
    return pl.BlockSpec(shape, lambda *_: (0,) * nd, pipeline_mode=pl.Buffered(1))


def _rows(tm, width):
    return pl.BlockSpec((None, tm, width), lambda b, i: (b, i, 0))


def _stream_rows(dil, tm, width):
    return pl.BlockSpec((None, dil, tm // dil, width), lambda b, i: (b, 0, i, 0))


def _rms(x, g):
    ms = jnp.mean(x * x, axis=-1, keepdims=True)
    return x * lax.rsqrt(ms + EPS) * g


def _silu(g):
    return g / (1.0 + jnp.exp2(g * -math.log2(math.e)))


def _shift_rows(cur, prev, k):
    rolled = pltpu.roll(cur, k, axis=0)
    prev_rolled = pltpu.roll(prev, k, axis=0)
    row = lax.broadcasted_iota(jnp.int32, prev.shape, 0)
    head = jnp.where(row < k, prev_rolled, rolled[:SUBLANES])
    return jnp.concatenate([head, rolled[SUBLANES:]], axis=0)


def _causal_conv3(cur, prev, w):
    return (_shift_rows(cur, prev, 2) * w[0:1] + _shift_rows(cur, prev, 1) * w[1:2]
            + cur * w[2:3])


_MAX_CHEAP_ROW_STRIDE = 4


def _store_streams(t, slab_ref, out_refs):
    tm = t.shape[0]
    n_slabs = A_WIDTH // LANES
    assert DILATIONS[0] == 1
    out_refs[0][0] = t.astype(BF16)
    for s in range(n_slabs):
        slab_ref[0, s] = t[:, s * LANES:(s + 1) * LANES]
    for level in range(1, len(DILATIONS)):
        prev, dil = DILATIONS[level - 1], DILATIONS[level]
        ratio = dil // prev
        assert ratio * prev == dil and ratio <= _MAX_CHEAP_ROW_STRIDE
        src, dst = slab_ref.at[(level - 1) % 2], slab_ref.at[level % 2]
        n = tm // dil
        for p in range(prev):
            for a in range(ratio):
                r = p + prev * a
                for s in range(n_slabs):
                    rows = src[s, pl.ds(p * (tm // prev) + a, n, stride=ratio), :]
                    out_refs[level][r, :, s * LANES:(s + 1) * LANES] = rows.astype(BF16)
                    if level + 1 < len(DILATIONS):
                        dst[s, r * n:(r + 1) * n, :] = rows


def _inproj_kernel(x_ref, g_ref, w_ref, qg_ref, kg_ref, hsum_ref, *rest):
    nd = len(DILATIONS)
    q_refs, k_refs, v_refs = rest[:nd], rest[nd:2 * nd], rest[2 * nd:3 * nd]
    slab_ref = rest[3 * nd]

    xn = _rms(x_ref[...], g_ref[...]).astype(BF16)

    def proj(idx):
        return jnp.dot(xn, w_ref[:, idx * A_WIDTH:(idx + 1) * A_WIDTH], preferred_element_type=F32)

    def head_norm(t, gain):
        t2 = (t * t).astype(BF16)
        half = A_WIDTH // 2
        ss = jnp.concatenate(
            [jnp.dot(t2[:, :half], hsum_ref[...], preferred_element_type=F32),
             jnp.dot(t2[:, half:], hsum_ref[...], preferred_element_type=F32)], axis=1)
        return t * lax.rsqrt(ss * (1.0 / A_HEAD_DIM) + EPS) * gain

    _store_streams(head_norm(proj(0), qg_ref[...]), slab_ref.at[0], q_refs)
    _store_streams(head_norm(proj(1), kg_ref[...]), slab_ref.at[1], k_refs)
    _store_streams(proj(2), slab_ref.at[2], v_refs)


def _inproj(x, g, w_qkv, q_gain, k_gain, hsum):
    B, S, D = x.shape
    tm = ROW_TILE
    streams = [jax.ShapeDtypeStruct((B, d, S // d, A_WIDTH), BF16) for d in DILATIONS]
    stream_specs = [_stream_rows(d, tm, A_WIDTH) for d in DILATIONS]
    outs = pl.pallas_call(
        _inproj_kernel,
        grid=(B, S // tm),
        in_specs=[_rows(tm, D), _resident((1, D)), _resident(w_qkv.shape), _resident((1, A_WIDTH)),
                  _resident((1, A_WIDTH)), _resident(hsum.shape)],
        out_specs=stream_specs * 3,
        out_shape=streams * 3,
        scratch_shapes=[pltpu.VMEM((3, 2, A_WIDTH // LANES, tm, LANES), F32)],
        compiler_params=_params("parallel", "parallel"),
        name="even_inproj",
    )(x, g, w_qkv, q_gain, k_gain, hsum)
    nd = len(DILATIONS)
    return outs[:nd], outs[nd:2 * nd], outs[2 * nd:3 * nd]


_Q_SCALE = A_HEAD_DIM ** -0.5 * math.log2(math.e)


def _col_reduce(x, op):
    parts = x.reshape(4, x.shape[0] // 4, x.shape[1])
    part = op(op(parts[0], parts[1]), op(parts[2], parts[3]))
    reduce = jnp.max if op is jnp.maximum else jnp.sum
    return reduce(part, axis=0, keepdims=True)


def _attn_kernel(q_ref, kp_ref, kc_ref, vp_ref, vc_ref, o_ref, lse_ref,
                 k_scr, vwin_scr, bias_scr, s_scr, ot_scr, lse_scr):
    first_tile = pl.program_id(2) == 0
    hd = A_HEAD_DIM
    n_pairs = A_WIDTH // LANES
    n_streams, tq = q_ref.shape[0], q_ref.shape[1]
    blocks_per_stream = tq // REACH
    n_blocks = n_streams * blocks_per_stream
    assert blocks_per_stream & (blocks_per_stream - 1) == 0 and n_blocks % 2 == 0

    kj = lax.broadcasted_iota(jnp.int32, (2 * REACH, 2 * REACH), 0)
    qi = lax.broadcasted_iota(jnp.int32, (2 * REACH, 2 * REACH), 1) % REACH
    band = (kj >= qi) & (kj <= qi + REACH)
    bias_scr[0] = jnp.where(band & ((kj >= REACH) | jnp.logical_not(first_tile)), 0.0, -1e30)
    bias_scr[1] = jnp.where(band, 0.0, -1e30)
    for s in range(n_streams):
        v_t = jnp.concatenate([vp_ref[s], vc_ref[s]], axis=0).T
        for j in range(blocks_per_stream):
            vwin_scr[s * blocks_per_stream + j] = v_t[:, j * REACH:(j + 2) * REACH]
        for pr in range(n_pairs):
            slab = slice(pr * LANES, (pr + 1) * LANES)
            k_scr[pr, s, 0:REACH, :] = kp_ref[s, :, slab]
            k_scr[pr, s, REACH:, :] = kc_ref[s, :, slab]
    low_half = lax.broadcasted_iota(jnp.int32, (REACH, LANES), 1) < hd
    zero = jnp.zeros((), BF16)
    ones_rows = jnp.ones((2 * SUBLANES, 2 * REACH), BF16)

    def block_scores(blk, slot):
        s = lax.shift_right_logical(blk, blocks_per_stream.bit_length() - 1)
        j = blk & (blocks_per_stream - 1)
        bias = bias_scr[jnp.minimum(j, 1)]
        maxima = []
        for pr in range(n_pairs):
            k_win = k_scr[pr, s, pl.ds(pl.multiple_of(j * REACH, REACH), 2 * REACH), :]
            q_slab = q_ref[s, pl.ds(pl.multiple_of(j * REACH, REACH), REACH),
                           pr * LANES:(pr + 1) * LANES]
            q_two = jnp.concatenate([jnp.where(low_half, q_slab, zero),
                                     jnp.where(low_half, zero, q_slab)], axis=0)
            sc = lax.dot_general(k_win, q_two, (((1,), (1,)), ((), ())),
                                 preferred_element_type=F32) + bias
            s_scr[slot, pr] = sc
            maxima.append(_col_reduce(sc, jnp.maximum))
        return tuple(maxima)

    def block_finish(blk, slot, maxima):
        for pr in range(n_pairs):
            p_t = jnp.exp2(s_scr[slot, pr] - maxima[pr]).astype(BF16)
            v_pair = vwin_scr[blk, pr * LANES:(pr + 1) * LANES, :]
            o_aug = jnp.dot(jnp.concatenate([v_pair, ones_rows], axis=0), p_t,
                            preferred_element_type=F32)
            den = o_aug[2 * hd:2 * hd + 1, :]
            o_t = o_aug[:2 * hd, :] / den
            ot_scr[blk, pr * LANES:(pr + 1) * LANES, :] = jnp.concatenate(
                [o_t[:hd, :REACH], o_t[hd:, REACH:]], axis=0).astype(BF16)
            lse = (maxima[pr] + jnp.log2(den)) * math.log(2.0)
            lse_scr[blk, pr] = jnp.broadcast_to(lse, (SUBLANES, 2 * REACH))

    def two_blocks(u, maxima_even, lookahead=True):
        blk = 2 * u
        maxima_odd = block_scores(blk + 1, 1)
        block_finish(blk, 0, maxima_even)
        maxima_next = block_scores(blk + 2, 0) if lookahead else None
        block_finish(blk + 1, 1, maxima_odd)
        return maxima_next

    maxima = lax.fori_loop(0, n_blocks // 2 - 1, two_blocks, block_scores(jnp.int32(0), 0))
    two_blocks(jnp.int32(n_blocks // 2 - 1), maxima, lookahead=False)

    for blk in range(n_blocks):
        s, j = divmod(blk, blocks_per_stream)
        rows = slice(j * REACH, (j + 1) * REACH)
        o_ref[s, rows, :] = ot_scr[blk].T
        lse_rows = []
        for pr in range(n_pairs):
            lse_rows += [lse_scr[blk, pr, 0:1, :REACH], lse_scr[blk, pr, 0:1, REACH:]]
        lse_rows.append(jnp.zeros((LANES - A_HEADS, REACH), F32))
        lse_ref[s, rows, :] = jnp.concatenate(lse_rows, axis=0).T


def _dilated_attention_one(q, k, v, dil):
    B, _, L, W = q.shape
    tq = min(ATTN_ROWS, L)
    ns = min(ATTN_ROWS // tq, dil)
    n_blocks, n_pairs = ns * tq // REACH, W // LANES
    cur = pl.BlockSpec((None, ns, tq, W), lambda b, r, i: (b, r, i, 0))
    prev = pl.BlockSpec((None, ns, REACH, W),
                        lambda b, r, i: (b, r, jnp.maximum(i * (tq // REACH) - 1, 0), 0))
    return pl.pallas_call(
        _attn_kernel,
        grid=(B, dil // ns, L // tq),
        in_specs=[cur, prev, cur, prev, cur],
        out_specs=[cur, pl.BlockSpec((None, ns, tq, LANES), lambda b, r, i: (b, r, i, 0))],
        out_shape=[jax.ShapeDtypeStruct((B, dil, L, W), BF16),
                   jax.ShapeDtypeStruct((B, dil, L, LANES), F32)],
        scratch_shapes=[
            pltpu.VMEM((n_pairs, ns, REACH + tq, LANES), BF16),
            pltpu.VMEM((n_blocks, W, 2 * REACH), BF16),
            pltpu.VMEM((2, 2 * REACH, 2 * REACH), F32),
            pltpu.VMEM((2, n_pairs, 2 * REACH, 2 * REACH), F32),
            pltpu.VMEM((n_blocks, W, REACH), BF16),
            pltpu.VMEM((n_blocks, n_pairs, SUBLANES, 2 * REACH), F32),
        ],
        compiler_params=_params("parallel", "parallel", "arbitrary"),
        name=f"dilated_attn_d{dil}",
    )(q, k, k, v, v)


def _load_streams(src_ref, dil, slab_ref):
    n_slabs = src_ref.shape[-1] // LANES
    if dil == 1:
        t = src_ref[0].astype(F32)
        return [t[:, s * LANES:(s + 1) * LANES] for s in range(n_slabs)]
    n = src_ref.shape[1]
    for r in range(dil):
        for s in range(n_slabs):
            slab_ref[s, pl.ds(r, n, stride=dil), :] = (
                src_ref[r, :, s * LANES:(s + 1) * LANES].astype(F32))
    return [slab_ref[s] for s in range(n_slabs)]


def _outproj_kernel(x_ref, *rest):
    nd = len(DILATIONS)
    o_refs, l_refs = rest[:nd], rest[nd:2 * nd]
    (g_ref, wb_ref, cw_ref, expand_ref, w_ref, out_ref,
     o_slab_ref, l_slab_ref, halo_ref) = rest[2 * nd:]
    n_slabs = A_WIDTH // LANES

    @pl.when(pl.program_id(1) == 0)
    def _():
        halo_ref[...] = jnp.zeros_like(halo_ref)

    x = x_ref[...]
    xn = _rms(x, g_ref[...]).astype(BF16)

    def proj(idx):
        return jnp.dot(xn, wb_ref[:, idx * B_WIDTH:(idx + 1) * B_WIDTH], preferred_element_type=F32)

    gate_b = proj(0)
    s = proj(1) * proj(2)
    conv = _causal_conv3(s, halo_ref[...], cw_ref[...])
    halo_ref[...] = s[s.shape[0] - SUBLANES:]
    b = (gate_b * conv).astype(BF16)

    ls = [_load_streams(l_ref, d, l_slab_ref.at[n])[0]
          for n, (d, l_ref) in enumerate(zip(DILATIONS, l_refs))]
    mx = functools.reduce(jnp.maximum, ls)
    es = [jnp.exp(l - mx) for l in ls]
    tot = functools.reduce(jnp.add, es)
    a_slabs = [None] * n_slabs
    for n, (d, e, o_ref) in enumerate(zip(DILATIONS, es, o_refs)):
        w = e / tot
        wf = jnp.dot(w.astype(BF16), expand_ref[...], preferred_element_type=F32)
        o_slabs = _load_streams(o_ref, d, o_slab_ref.at[n])
        for s in range(n_slabs):
            term = wf[:, s * LANES:(s + 1) * LANES] * o_slabs[s]
            a_slabs[s] = term if a_slabs[s] is None else a_slabs[s] + term
    a = jnp.concatenate(a_slabs, axis=1).astype(BF16)
    y = (jnp.dot(a, w_ref[:A_WIDTH, :], preferred_element_type=F32)
         + jnp.dot(b, w_ref[A_WIDTH:, :], preferred_element_type=F32))
    out_ref[...] = x + y


def _outproj(x, os, lses, g, w_bch, sconv_w, expand, w_out):
    B, S, D = x.shape
    tm = ROW_TILE
    nd = len(DILATIONS)
    return pl.pallas_call(
        _outproj_kernel,
        grid=(B, S // tm),
        in_specs=[_rows(tm, D)] + [_stream_rows(d, tm, A_WIDTH) for d in DILATIONS]
                 + [_stream_rows(d, tm, LANES) for d in DILATIONS]
                 + [_resident((1, D)), _resident(w_bch.shape), _resident(sconv_w.shape),
                    _resident(expand.shape), _resident(w_out.shape)],
        out_specs=_rows(tm, D),
        out_shape=jax.ShapeDtypeStruct((B, S, D), F32),
        scratch_shapes=[pltpu.VMEM((nd, A_WIDTH // LANES, tm, LANES), F32),
                        pltpu.VMEM((nd, 1, tm, LANES), F32),
                        pltpu.VMEM((SUBLANES, B_WIDTH), F32)],
        compiler_params=_params("parallel", "arbitrary"),
        name="even_outproj",
    )(x, *os, *lses, g, w_bch, sconv_w, expand, w_out)


def _ffn_kernel(x_ref, g_ref, wup_ref, cw_ref, wdn_ref, o_ref, h_ref, halo_ref):
    @pl.when(pl.program_id(1) == 0)
    def _():
        halo_ref[...] = jnp.zeros_like(halo_ref)

    x = x_ref[...]
    tm = x.shape[0]
    xn = _rms(x, g_ref[...]).astype(BF16)

    def conv_up(col):
        cols = slice(col, col + FF_TILE)
        u = jnp.dot(xn, wup_ref[:, cols], preferred_element_type=F32)
        y = _causal_conv3(u, halo_ref[:, cols], cw_ref[:, cols])
        halo_ref[:, cols] = u[tm - SUBLANES:]
        return y

    for c in range(D_FF // FF_TILE):
        gate = conv_up(c * FF_TILE)
        val = conv_up(D_FF + c * FF_TILE)
        h_ref[:, c * FF_TILE:(c + 1) * FF_TILE] = (_silu(gate) * val).astype(BF16)
    o_ref[...] = x + jnp.dot(h_ref[...], wdn_ref[...], preferred_element_type=F32)


def _ffn(x, g, w_up, conv_w, w_down):
    B, S, D = x.shape
    tm = FFN_ROW_TILE
    return pl.pallas_call(
        _ffn_kernel,
        grid=(B, S // tm),
        in_specs=[_rows(tm, D), _resident((1, D)), _resident(w_up.shape), _resident(conv_w.shape),
                  _resident(w_down.shape)],
        out_specs=_rows(tm, D),
        out_shape=jax.ShapeDtypeStruct((B, S, D), F32),
        scratch_shapes=[pltpu.VMEM((tm, D_FF), BF16), pltpu.VMEM((SUBLANES, 2 * D_FF), F32)],
        compiler_params=_params("parallel", "arbitrary"),
        name="conv_glu_ffn",
    )(x, g, w_up, conv_w, w_down)


def _rope_table_kernel(cos_ref, sin_ref):
    half = cos_ref.shape[1]
    pos = lax.broadcasted_iota(jnp.int32, cos_ref.shape, 0).astype(F32)
    j = lax.broadcasted_iota(jnp.int32, cos_ref.shape, 1).astype(F32)
    inv = jnp.exp(j * (-math.log(ROPE_BASE) / half))
    ang = pos * inv
    cos_ref[...] = jnp.cos(ang)
    sin_ref[...] = jnp.sin(ang)


def _rope_table(S):
    half = RET_KDIM // 2
    shape = jax.ShapeDtypeStruct((S, half), F32)
    return pl.pallas_call(_rope_table_kernel, out_shape=[shape, shape], name="rope_table")()


def _retproj_kernel(x_ref, g_ref, wq_ref, wk_ref, wg_ref, cos_ref, sin_ref,
                    q_ref, k_ref, xn_ref, gate_ref):
    xn = _rms(x_ref[...], g_ref[...]).astype(BF16)
    xn_ref[...] = xn
    cos = cos_ref[...]
    sin = sin_ref[...]
    half = RET_KDIM // 2

    def rotary_store(t, dst_ref):
        for h in range(RET_HEADS):
            x1 = t[:, h * RET_KDIM:h * RET_KDIM + half]
            x2 = t[:, h * RET_KDIM + half:(h + 1) * RET_KDIM]
            dst_ref[:, h * RET_KDIM:h * RET_KDIM + half] = (x1 * cos - x2 * sin).astype(BF16)
            dst_ref[:, h * RET_KDIM + half:(h + 1) * RET_KDIM] = (x1 * sin + x2 * cos).astype(BF16)

    rotary_store(jnp.dot(xn, wq_ref[...], preferred_element_type=F32), q_ref)
    rotary_store(jnp.dot(xn, wk_ref[...], preferred_element_type=F32) * (RET_KDIM ** -0.5), k_ref)
    gate_ref[...] = _silu(jnp.dot(xn, wg_ref[...], preferred_element_type=F32)).astype(BF16)


def _retproj(x, g, wq, wk, wg, cos, sin):
    B, S, D = x.shape
    tm = ROW_TILE
    kw, vw = RET_HEADS * RET_KDIM, RET_HEADS * RET_VDIM
    table = pl.BlockSpec((tm, RET_KDIM // 2), lambda b, i: (i, 0))
    return pl.pallas_call(
        _retproj_kernel,
        grid=(B, S // tm),
        in_specs=[_rows(tm, D), _resident((1, D)), _resident(wq.shape), _resident(wk.shape),
                  _resident(wg.shape), table, table],
        out_specs=[_rows(tm, kw), _rows(tm, kw), _rows(tm, D), _rows(tm, vw)],
        out_shape=[jax.ShapeDtypeStruct((B, S, kw), BF16), jax.ShapeDtypeStruct((B, S, kw), BF16),
                   jax.ShapeDtypeStruct((B, S, D), BF16), jax.ShapeDtypeStruct((B, S, vw), BF16)],
        compiler_params=_params("parallel", "parallel"),
        name="ret_proj",
    )(x, g, wq, wk, wg, cos, sin)


def _retention_kernel(x_ref, q_ref, k_ref, xn_ref, gate_ref, wv_ref, wo_ref, o_ref, state_ref,
                      decay_ref, a_ref, v_ref):
    C = RET_CHUNK
    log_gs = [math.log1p(-(2.0 ** (-5.0 - h))) for h in range(RET_HEADS)]

    @pl.when((pl.program_id(0) == 0) & (pl.program_id(1) == 0))
    def _():
        ii = lax.broadcasted_iota(jnp.int32, (C, C), 0)
        jj = lax.broadcasted_iota(jnp.int32, (C, C), 1)
        diff = (ii - jj).astype(F32)
        for h in range(RET_HEADS):
            decay_ref[h] = jnp.where(diff >= 0, jnp.exp(log_gs[h] * jnp.maximum(diff, 0.0)), 0.0)

    @pl.when(pl.program_id(1) == 0)
    def _():
        state_ref[...] = jnp.zeros_like(state_ref)

    pos = lax.broadcasted_iota(jnp.int32, (C, 1), 0).astype(F32)
    q_decays = [jnp.exp(lg * (pos + 1.0)) for lg in log_gs]
    k_decays = [jnp.exp(lg * (C - 1.0 - pos)) for lg in log_gs]

    for c in range(q_ref.shape[0] // C):
        rows = slice(c * C, (c + 1) * C)
        v_ref[rows, :] = jnp.dot(xn_ref[rows, :], wv_ref[...],
                                 preferred_element_type=F32).astype(BF16)
        for h in range(RET_HEADS):
            kcols = slice(h * RET_KDIM, (h + 1) * RET_KDIM)
            vcols = slice(h * RET_VDIM, (h + 1) * RET_VDIM)
            qc = q_ref[rows, kcols]
            kc = k_ref[rows, kcols]
            vc = v_ref[rows, vcols]
            state = state_ref[h]
            scores = lax.dot_general(qc, kc, (((1,), (1,)), ((), ())),
                                     preferred_element_type=F32) * decay_ref[h]
            o = (jnp.dot(scores.astype(BF16), vc, preferred_element_type=F32)
                 + jnp.dot(qc, state.astype(BF16), preferred_element_type=F32) * q_decays[h])
            kd = (kc.astype(F32) * k_decays[h]).astype(BF16)
            state_ref[h] = state * math.exp(log_gs[h] * C) + lax.dot_general(
                kd, vc, (((0,), (0,)), ((), ())), preferred_element_type=F32)
            mu = jnp.mean(o, axis=-1, keepdims=True)
            var = jnp.mean(jnp.square(o - mu), axis=-1, keepdims=True)
            on = (o - mu) * lax.rsqrt(var + EPS)
            a_ref[rows, vcols] = (gate_ref[rows, vcols].astype(F32) * on).astype(BF16)
        o_ref[rows, :] = x_ref[rows, :] + jnp.dot(a_ref[rows, :], wo_ref[...],
                                                  preferred_element_type=F32)


def _retention(x, q, k, xn, gate, wv, wo):
    B, S, D = x.shape
    kw, vw = q.shape[-1], gate.shape[-1]
    tc = RET_TILE
    return pl.pallas_call(
        _retention_kernel,
        grid=(B, S // tc),
        in_specs=[_rows(tc, D), _rows(tc, kw), _rows(tc, kw), _rows(tc, D), _rows(tc, vw),
                  _resident(wv.shape), _resident(wo.shape)],
        out_specs=_rows(tc, D),
        out_shape=jax.ShapeDtypeStruct((B, S, D), F32),
        scratch_shapes=[pltpu.VMEM((RET_HEADS, RET_KDIM, RET_VDIM), F32),
                        pltpu.VMEM((RET_HEADS, RET_CHUNK, RET_CHUNK), F32),
                        pltpu.VMEM((tc, vw), BF16), pltpu.VMEM((tc, vw), BF16)],
        compiler_params=_params("arbitrary", "arbitrary"),
        name="retention",
    )(x, q, k, xn, gate, wv, wo)


def _head_sum_matrix():
    g = jnp.arange(A_WIDTH // 2) // A_HEAD_DIM
    return (g[:, None] == g[None, :]).astype(BF16)


def _head_expand_matrix():
    head_of_lane = jnp.arange(A_WIDTH) // A_HEAD_DIM
    return (jnp.arange(LANES)[:, None] == head_of_lane[None, :]).astype(BF16)


def kernel(x, even_norm, even_w_in, even_q_gain, even_k_gain, even_sconv_w, even_w_out, odd_norm,
           ret_wq, ret_wk, ret_wv, ret_wg, ret_gn_gain, ret_wo, ffn_norm, ffn_w_up, ffn_conv_w,
           ffn_w_down):
    B, S, D = x.shape
    depth = ffn_norm.shape[0]
    bf = lambda w: w.astype(BF16)
    row = lambda v: v.reshape(1, -1).astype(F32)
    hsum = _head_sum_matrix()
    expand = _head_expand_matrix()
    cos = sin = None

    for l in range(depth):
        if l % 2 == 0:
            e = l // 2
            n_qkv = 3 * A_WIDTH
            qs, ks, vs = _inproj(x, row(even_norm[e]), bf(even_w_in[e][:, :n_qkv]),
                                 row(jnp.tile(even_q_gain[e], A_HEADS) * _Q_SCALE),
                                 row(jnp.tile(even_k_gain[e], A_HEADS)), hsum)
            os, lses = zip(*[_dilated_attention_one(q, k, v, d)
                             for q, k, v, d in zip(qs, ks, vs, DILATIONS)])
            x = _outproj(x, os, lses, row(even_norm[e]), bf(even_w_in[e][:, n_qkv:]),
                         even_sconv_w[e], expand, bf(even_w_out[e]))
        else:
            o = l // 2
            if cos is None:
                cos, sin = _rope_table(S)
            q, k, xn, gate = _retproj(x, row(odd_norm[o]), bf(ret_wq[o]), bf(ret_wk[o]),
                                      bf(ret_wg[o]), cos, sin)
            x = _retention(x, q, k, xn, gate, bf(ret_wv[o]),
                           bf(ret_gn_gain[o].reshape(-1, 1) * ret_wo[o]))
        x = _ffn(x, row(ffn_norm[l]), bf(ffn_w_up[l]), ffn_conv_w[l], bf(ffn_w_down[l]))
    return x
```

```python
import functools
import math

import jax
import jax.numpy as jnp
from jax import lax
from jax.experimental import pallas as pl
from jax.experimental.pallas import tpu as pltpu

F32 = jnp.float32
BF16 = jnp.bfloat16

D_MODEL = 1024
A_HEADS = 8
A_HEAD_DIM = 64
A_WIDTH = A_HEADS * A_HEAD_DIM
B_WIDTH = D_MODEL - A_WIDTH
DILATIONS = (1, 4, 16)
REACH = 128
RET_HEADS = 4
RET_KDIM = D_MODEL // RET_HEADS
RET_VDIM = 2 * D_MODEL // RET_HEADS
RET_CHUNK = 256
ROPE_BASE = 10000.0
D_FF = 2816
EPS = 1e-6

SUBLANES = 8
LANES = 128
VMEM_LIMIT_BYTES = 56 * 1024 * 1024

ROW_TILE = 1024
FFN_ROW_TILE = 1024
FF_TILE = 256
ATTN_ROWS = 2048
RET_TILE = 512


def _params(*semantics):
    return pltpu.CompilerParams(dimension_semantics=semantics, vmem_limit_bytes=VMEM_LIMIT_BYTES)


def _resident(shape):
    nd = len(shape)
    return pl.BlockSpec(shape, lambda *_: (0,) * nd, pipeline_mode=pl.Buffered(1))


def _rows(tm, width):
    return pl.BlockSpec((None, tm, width), lambda b, i: (b, i, 0))


def _stream_rows(dil, tm, width):
    return pl.BlockSpec((None, dil, tm // dil, width), lambda b, i: (b, 0, i, 0))


def _rms(x, g):
    ms = jnp.mean(x * x, axis=-1, keepdims=True)
    return x * lax.rsqrt(ms + EPS) * g


def _silu(g):
    return g / (1.0 + jnp.exp2(g * -math.log2(math.e)))


def _shift_rows(cur, prev, k):
    rolled = pltpu.roll(cur, k, axis=0)
    prev_rolled = pltpu.roll(prev, k, axis=0)
    row = lax.broadcasted_iota(jnp.int32, prev.shape, 0)
    head = jnp.where(row < k, prev_rolled, rolled[:SUBLANES])
    return jnp.concatenate([head, rolled[SUBLANES:]], axis=0)


def _causal_conv3(cur, prev, w):
    return (_shift_rows(cur, prev, 2) * w[0:1] + _shift_rows(cur, prev, 1) * w[1:2]
            + cur * w[2:3])


def _stream_group(dil):
    return max(LANES, 2 * SUBLANES * dil)


def _stream_perm_matrix(dil):
    group = _stream_group(dil)
    dst = jnp.arange(group)
    src = (dst % (group // dil)) * dil + dst // (group // dil)
    return (jnp.arange(group)[None, :] == src[:, None]).astype(BF16)


def _store_streams(t, perm_refs, out_refs):
    tm = t.shape[0]
    tb = t.astype(BF16)
    for dil, perm_ref, out_ref in zip(DILATIONS, perm_refs, out_refs):
        if dil == 1:
            out_ref[0] = tb
            continue
        group = perm_ref.shape[0]
        n = group // dil
        for g in range(tm // group):
            y = jnp.dot(perm_ref[...], tb[g * group:(g + 1) * group, :],
                        preferred_element_type=F32).astype(BF16)
            for r in range(dil):
                out_ref[r, g * n:(g + 1) * n, :] = y[r * n:(r + 1) * n, :]


def _inproj_kernel(x_ref, g_ref, w_ref, qg_ref, kg_ref, hsum_ref, *rest):
    nd = len(DILATIONS)
    perm_refs, rest = rest[:nd], rest[nd:]
    q_refs, k_refs, v_refs = rest[:nd], rest[nd:2 * nd], rest[2 * nd:3 * nd]

    xn = _rms(x_ref[...], g_ref[...]).astype(BF16)

    def proj(idx):
        return jnp.dot(xn, w_ref[:, idx * A_WIDTH:(idx + 1) * A_WIDTH], preferred_element_type=F32)

    def head_norm(t, gain):
        t2 = (t * t).astype(BF16)
        half = A_WIDTH // 2
        ss = jnp.concatenate(
            [jnp.dot(t2[:, :half], hsum_ref[...], preferred_element_type=F32),
             jnp.dot(t2[:, half:], hsum_ref[...], preferred_element_type=F32)], axis=1)
        return t * lax.rsqrt(ss * (1.0 / A_HEAD_DIM) + EPS) * gain

    _store_streams(head_norm(proj(0), qg_ref[...]), perm_refs, q_refs)
    _store_streams(head_norm(proj(1), kg_ref[...]), perm_refs, k_refs)
    _store_streams(proj(2), perm_refs, v_refs)


def _inproj(x, g, w_qkv, q_gain, k_gain, hsum):
    B, S, D = x.shape
    tm = ROW_TILE
    streams = [jax.ShapeDtypeStruct((B, d, S // d, A_WIDTH), BF16) for d in DILATIONS]
    stream_specs = [_stream_rows(d, tm, A_WIDTH) for d in DILATIONS]
    perms = [_stream_perm_matrix(d) for d in DILATIONS]
    outs = pl.pallas_call(
        _inproj_kernel,
        grid=(B, S // tm),
        in_specs=[_rows(tm, D), _resident((1, D)), _resident(w_qkv.shape), _resident((1, A_WIDTH)),
                  _resident((1, A_WIDTH)), _resident(hsum.shape)]
                 + [_resident(p.shape) for p in perms],
        out_specs=stream_specs * 3,
        out_shape=streams * 3,
        compiler_params=_params("parallel", "parallel"),
        name="even_inproj",
    )(x, g, w_qkv, q_gain, k_gain, hsum, *perms)
    nd = len(DILATIONS)
    return outs[:nd], outs[nd:2 * nd], outs[2 * nd:3 * nd]


_Q_SCALE = A_HEAD_DIM ** -0.5 * math.log2(math.e)


def _col_reduce(x, op):
    parts = x.reshape(4, x.shape[0] // 4, x.shape[1])
    part = op(op(parts[0], parts[1]), op(parts[2], parts[3]))
    reduce = jnp.max if op is jnp.maximum else jnp.sum
    return reduce(part, axis=0, keepdims=True)


def _attn_kernel(q_ref, kp_ref, kc_ref, vp_ref, vc_ref, o_ref, lse_ref,
                 k_scr, vwin_scr, bias_scr, s_scr, ot_scr, lse_scr):
    first_tile = pl.program_id(2) == 0
    hd = A_HEAD_DIM
    n_pairs = A_WIDTH // LANES
    n_streams, tq = q_ref.shape[0], q_ref.shape[1]
    blocks_per_stream = tq // REACH
    n_blocks = n_streams * blocks_per_stream
    assert blocks_per_stream & (blocks_per_stream - 1) == 0 and n_blocks % 2 == 0

    kj = lax.broadcasted_iota(jnp.int32, (2 * REACH, 2 * REACH), 0)
    qi = lax.broadcasted_iota(jnp.int32, (2 * REACH, 2 * REACH), 1) % REACH
    band = (kj >= qi) & (kj <= qi + REACH)
    bias_scr[0] = jnp.where(band & ((kj >= REACH) | jnp.logical_not(first_tile)), 0.0, -1e30)
    bias_scr[1] = jnp.where(band, 0.0, -1e30)
    for s in range(n_streams):
        v_t = jnp.concatenate([vp_ref[s], vc_ref[s]], axis=0).T
        for j in range(blocks_per_stream):
            vwin_scr[s * blocks_per_stream + j] = v_t[:, j * REACH:(j + 2) * REACH]
        for pr in range(n_pairs):
            slab = slice(pr * LANES, (pr + 1) * LANES)
            k_scr[pr, s, 0:REACH, :] = kp_ref[s, :, slab]
            k_scr[pr, s, REACH:, :] = kc_ref[s, :, slab]
    low_half = lax.broadcasted_iota(jnp.int32, (REACH, LANES), 1) < hd
    zero = jnp.zeros((), BF16)
    ones_rows = jnp.ones((2 * SUBLANES, 2 * REACH), BF16)

    def block_scores(blk, slot):
        s = lax.shift_right_logical(blk, blocks_per_stream.bit_length() - 1)
        j = blk & (blocks_per_stream - 1)
        bias = bias_scr[jnp.minimum(j, 1)]
        maxima = []
        for pr in range(n_pairs):
            k_win = k_scr[pr, s, pl.ds(pl.multiple_of(j * REACH, REACH), 2 * REACH), :]
            q_slab = q_ref[s, pl.ds(pl.multiple_of(j * REACH, REACH), REACH),
                           pr * LANES:(pr + 1) * LANES]
            q_two = jnp.concatenate([jnp.where(low_half, q_slab, zero),
                                     jnp.where(low_half, zero, q_slab)], axis=0)
            sc = lax.dot_general(k_win, q_two, (((1,), (1,)), ((), ())),
                                 preferred_element_type=F32) + bias
            s_scr[slot, pr] = sc
            maxima.append(_col_reduce(sc, jnp.maximum))
        return tuple(maxima)

    def block_finish(blk, slot, maxima):
        for pr in range(n_pairs):
            p_t = jnp.exp2(s_scr[slot, pr] - maxima[pr]).astype(BF16)
            v_pair = vwin_scr[blk, pr * LANES:(pr + 1) * LANES, :]
            o_aug = jnp.dot(jnp.concatenate([v_pair, ones_rows], axis=0), p_t,
                            preferred_element_type=F32)
            den = o_aug[2 * hd:2 * hd + 1, :]
            o_t = o_aug[:2 * hd, :] / den
            ot_scr[blk, pr * LANES:(pr + 1) * LANES, :] = jnp.concatenate(
                [o_t[:hd, :REACH], o_t[hd:, REACH:]], axis=0).astype(BF16)
            lse = (maxima[pr] + jnp.log2(den)) * math.log(2.0)
            lse_scr[blk, pr] = jnp.broadcast_to(lse, (SUBLANES, 2 * REACH))

    n_slots = s_scr.shape[0]
    assert n_blocks % n_slots == 0

    def some_blocks(u, maxima, lookahead=True):
        blk = n_slots * u
        for i in range(n_slots):
            if i + 1 < n_slots:
                maxima_next = block_scores(blk + i + 1, i + 1)
            else:
                maxima_next = block_scores(blk + n_slots, 0) if lookahead else None
            block_finish(blk + i, i, maxima)
            maxima = maxima_next
        return maxima

    maxima = lax.fori_loop(0, n_blocks // n_slots - 1, some_blocks,
                           block_scores(jnp.int32(0), 0))
    some_blocks(jnp.int32(n_blocks // n_slots - 1), maxima, lookahead=False)

    for blk in range(n_blocks):
        s, j = divmod(blk, blocks_per_stream)
        rows = slice(j * REACH, (j + 1) * REACH)
        o_ref[s, rows, :] = ot_scr[blk].T
        lse_rows = []
        for pr in range(n_pairs):
            lse_rows += [lse_scr[blk, pr, 0:1, :REACH], lse_scr[blk, pr, 0:1, REACH:]]
        lse_rows.append(jnp.zeros((LANES - A_HEADS, REACH), F32))
        lse_ref[s, rows, :] = jnp.concatenate(lse_rows, axis=0).T


def _dilated_attention_one(q, k, v, dil):
    B, _, L, W = q.shape
    tq = min(ATTN_ROWS, L)
    ns = min(ATTN_ROWS // tq, dil)
    n_blocks, n_pairs = ns * tq // REACH, W // LANES
    cur = pl.BlockSpec((None, ns, tq, W), lambda b, r, i: (b, r, i, 0))
    prev = pl.BlockSpec((None, ns, REACH, W),
                        lambda b, r, i: (b, r, jnp.maximum(i * (tq // REACH) - 1, 0), 0))
    return pl.pallas_call(
        _attn_kernel,
        grid=(B, dil // ns, L // tq),
        in_specs=[cur, prev, cur, prev, cur],
        out_specs=[cur, pl.BlockSpec((None, ns, tq, LANES), lambda b, r, i: (b, r, i, 0))],
        out_shape=[jax.ShapeDtypeStruct((B, dil, L, W), BF16),
                   jax.ShapeDtypeStruct((B, dil, L, LANES), F32)],
        scratch_shapes=[
            pltpu.VMEM((n_pairs, ns, REACH + tq, LANES), BF16),
            pltpu.VMEM((n_blocks, W, 2 * REACH), BF16),
            pltpu.VMEM((2, 2 * REACH, 2 * REACH), F32),
            pltpu.VMEM((4, n_pairs, 2 * REACH, 2 * REACH), F32),
            pltpu.VMEM((n_blocks, W, REACH), BF16),
            pltpu.VMEM((n_blocks, n_pairs, SUBLANES, 2 * REACH), F32),
        ],
        compiler_params=_params("parallel", "parallel", "arbitrary"),
        name=f"dilated_attn_d{dil}",
    )(q, k, k, v, v)


def _load_streams(src_ref, dil, slab_ref):
    n_slabs = src_ref.shape[-1] // LANES
    if dil == 1:
        t = src_ref[0].astype(F32)
        return [t[:, s * LANES:(s + 1) * LANES] for s in range(n_slabs)]
    n = src_ref.shape[1]
    for r in range(dil):
        for s in range(n_slabs):
            slab_ref[s, pl.ds(r, n, stride=dil), :] = (
                src_ref[r, :, s * LANES:(s + 1) * LANES].astype(F32))
    return [slab_ref[s] for s in range(n_slabs)]


def _outproj_kernel(x_ref, *rest):
    nd = len(DILATIONS)
    o_refs, l_refs = rest[:nd], rest[nd:2 * nd]
    (g_ref, wb_ref, cw_ref, expand_ref, w_ref, out_ref,
     o_slab_ref, l_slab_ref, halo_ref) = rest[2 * nd:]
    n_slabs = A_WIDTH // LANES

    @pl.when(pl.program_id(1) == 0)
    def _():
        halo_ref[...] = jnp.zeros_like(halo_ref)

    x = x_ref[...]
    xn = _rms(x, g_ref[...]).astype(BF16)

    def proj(idx):
        return jnp.dot(xn, wb_ref[:, idx * B_WIDTH:(idx + 1) * B_WIDTH], preferred_element_type=F32)

    gate_b = proj(0)
    s = proj(1) * proj(2)
    conv = _causal_conv3(s, halo_ref[...], cw_ref[...])
    halo_ref[...] = s[s.shape[0] - SUBLANES:]
    b = (gate_b * conv).astype(BF16)

    ls = [_load_streams(l_ref, d, l_slab_ref.at[n])[0]
          for n, (d, l_ref) in enumerate(zip(DILATIONS, l_refs))]
    mx = functools.reduce(jnp.maximum, ls)
    es = [jnp.exp(l - mx) for l in ls]
    tot = functools.reduce(jnp.add, es)
    a_slabs = [None] * n_slabs
    for n, (d, e, o_ref) in enumerate(zip(DILATIONS, es, o_refs)):
        w = e / tot
        wf = jnp.dot(w.astype(BF16), expand_ref[...], preferred_element_type=F32)
        o_slabs = _load_streams(o_ref, d, o_slab_ref.at[n])
        for s in range(n_slabs):
            term = wf[:, s * LANES:(s + 1) * LANES] * o_slabs[s]
            a_slabs[s] = term if a_slabs[s] is None else a_slabs[s] + term
    a = jnp.concatenate(a_slabs, axis=1).astype(BF16)
    y = (jnp.dot(a, w_ref[:A_WIDTH, :], preferred_element_type=F32)
         + jnp.dot(b, w_ref[A_WIDTH:, :], preferred_element_type=F32))
    out_ref[...] = x + y


def _outproj(x, os, lses, g, w_bch, sconv_w, expand, w_out):
    B, S, D = x.shape
    tm = ROW_TILE
    nd = len(DILATIONS)
    return pl.pallas_call(
        _outproj_kernel,
        grid=(B, S // tm),
        in_specs=[_rows(tm, D)] + [_stream_rows(d, tm, A_WIDTH) for d in DILATIONS]
                 + [_stream_rows(d, tm, LANES) for d in DILATIONS]
                 + [_resident((1, D)), _resident(w_bch.shape), _resident(sconv_w.shape),
                    _resident(expand.shape), _resident(w_out.shape)],
        out_specs=_rows(tm, D),
        out_shape=jax.ShapeDtypeStruct((B, S, D), F32),
        scratch_shapes=[pltpu.VMEM((nd, A_WIDTH // LANES, tm, LANES), F32),
                        pltpu.VMEM((nd, 1, tm, LANES), F32),
                        pltpu.VMEM((SUBLANES, B_WIDTH), F32)],
        compiler_params=_params("parallel", "arbitrary"),
        name="even_outproj",
    )(x, *os, *lses, g, w_bch, sconv_w, expand, w_out)


def _ffn_kernel(x_ref, g_ref, wup_ref, cw_ref, wdn_ref, o_ref, h_ref, halo_ref):
    @pl.when(pl.program_id(1) == 0)
    def _():
        halo_ref[...] = jnp.zeros_like(halo_ref)

    x = x_ref[...]
    tm = x.shape[0]
    xn = _rms(x, g_ref[...]).astype(BF16)

    def conv_up(col):
        cols = slice(col, col + FF_TILE)
        u = jnp.dot(xn, wup_ref[:, cols], preferred_element_type=F32)
        y = _causal_conv3(u, halo_ref[:, cols], cw_ref[:, cols])
        halo_ref[:, cols] = u[tm - SUBLANES:]
        return y

    for c in range(D_FF // FF_TILE):
        gate = conv_up(c * FF_TILE)
        val = conv_up(D_FF + c * FF_TILE)
        h_ref[:, c * FF_TILE:(c + 1) * FF_TILE] = (_silu(gate) * val).astype(BF16)
    o_ref[...] = x + jnp.dot(h_ref[...], wdn_ref[...], preferred_element_type=F32)


def _ffn(x, g, w_up, conv_w, w_down):
    B, S, D = x.shape
    tm = FFN_ROW_TILE
    return pl.pallas_call(
        _ffn_kernel,
        grid=(B, S // tm),
        in_specs=[_rows(tm, D), _resident((1, D)), _resident(w_up.shape), _resident(conv_w.shape),
                  _resident(w_down.shape)],
        out_specs=_rows(tm, D),
        out_shape=jax.ShapeDtypeStruct((B, S, D), F32),
        scratch_shapes=[pltpu.VMEM((tm, D_FF), BF16), pltpu.VMEM((SUBLANES, 2 * D_FF), F32)],
        compiler_params=_params("parallel", "arbitrary"),
        name="conv_glu_ffn",
    )(x, g, w_up, conv_w, w_down)


def _rope_table_kernel(cos_ref, sin_ref):
    half = cos_ref.shape[1]
    pos = lax.broadcasted_iota(jnp.int32, cos_ref.shape, 0).astype(F32)
    j = lax.broadcasted_iota(jnp.int32, cos_ref.shape, 1).astype(F32)
    inv = jnp.exp(j * (-math.log(ROPE_BASE) / half))
    ang = pos * inv
    cos_ref[...] = jnp.cos(ang)
    sin_ref[...] = jnp.sin(ang)


def _rope_table(S):
    half = RET_KDIM // 2
    shape = jax.ShapeDtypeStruct((S, half), F32)
    return pl.pallas_call(_rope_table_kernel, out_shape=[shape, shape], name="rope_table")()


def _retproj_kernel(x_ref, g_ref, wq_ref, wk_ref, wg_ref, cos_ref, sin_ref,
                    q_ref, k_ref, xn_ref, gate_ref):
    xn = _rms(x_ref[...], g_ref[...]).astype(BF16)
    xn_ref[...] = xn
    cos = cos_ref[...]
    sin = sin_ref[...]
    half = RET_KDIM // 2

    def rotary_store(t, dst_ref):
        for h in range(RET_HEADS):
            x1 = t[:, h * RET_KDIM:h * RET_KDIM + half]
            x2 = t[:, h * RET_KDIM + half:(h + 1) * RET_KDIM]
            dst_ref[:, h * RET_KDIM:h * RET_KDIM + half] = (x1 * cos - x2 * sin).astype(BF16)
            dst_ref[:, h * RET_KDIM + half:(h + 1) * RET_KDIM] = (x1 * sin + x2 * cos).astype(BF16)

    rotary_store(jnp.dot(xn, wq_ref[...], preferred_element_type=F32), q_ref)
    rotary_store(jnp.dot(xn, wk_ref[...], preferred_element_type=F32) * (RET_KDIM ** -0.5), k_ref)
    gate_ref[...] = _silu(jnp.dot(xn, wg_ref[...], preferred_element_type=F32)).astype(BF16)


def _retproj(x, g, wq, wk, wg, cos, sin):
    B, S, D = x.shape
    tm = ROW_TILE
    kw, vw = RET_HEADS * RET_KDIM, RET_HEADS * RET_VDIM
    table = pl.BlockSpec((tm, RET_KDIM // 2), lambda b, i: (i, 0))
    return pl.pallas_call(
        _retproj_kernel,
        grid=(B, S // tm),
        in_specs=[_rows(tm, D), _resident((1, D)), _resident(wq.shape), _resident(wk.shape),
                  _resident(wg.shape), table, table],
        out_specs=[_rows(tm, kw), _rows(tm, kw), _rows(tm, D), _rows(tm, vw)],
        out_shape=[jax.ShapeDtypeStruct((B, S, kw), BF16), jax.ShapeDtypeStruct((B, S, kw), BF16),
                   jax.ShapeDtypeStruct((B, S, D), BF16), jax.ShapeDtypeStruct((B, S, vw), BF16)],
        compiler_params=_params("parallel", "parallel"),
        name="ret_proj",
    )(x, g, wq, wk, wg, cos, sin)


def _retention_kernel(x_ref, q_ref, k_ref, xn_ref, gate_ref, wv_ref, wo_ref, o_ref, state_ref,
                      decay_ref, a_ref, v_ref):
    C = RET_CHUNK
    log_gs = [math.log1p(-(2.0 ** (-5.0 - h))) for h in range(RET_HEADS)]

    @pl.when((pl.program_id(0) == 0) & (pl.program_id(1) == 0))
    def _():
        ii = lax.broadcasted_iota(jnp.int32, (C, C), 0)
        jj = lax.broadcasted_iota(jnp.int32, (C, C), 1)
        diff = (ii - jj).astype(F32)
        for h in range(RET_HEADS):
            decay_ref[h] = jnp.where(diff >= 0, jnp.exp(log_gs[h] * jnp.maximum(diff, 0.0)), 0.0)

    @pl.when(pl.program_id(1) == 0)
    def _():
        state_ref[...] = jnp.zeros_like(state_ref)

    pos = lax.broadcasted_iota(jnp.int32, (C, 1), 0).astype(F32)
    q_decays = [jnp.exp(lg * (pos + 1.0)) for lg in log_gs]
    k_decays = [jnp.exp(lg * (C - 1.0 - pos)) for lg in log_gs]

    for c in range(q_ref.shape[0] // C):
        rows = slice(c * C, (c + 1) * C)
        v_ref[rows, :] = jnp.dot(xn_ref[rows, :], wv_ref[...],
                                 preferred_element_type=F32).astype(BF16)
        for h in range(RET_HEADS):
            kcols = slice(h * RET_KDIM, (h + 1) * RET_KDIM)
            vcols = slice(h * RET_VDIM, (h + 1) * RET_VDIM)
            qc = q_ref[rows, kcols]
            kc = k_ref[rows, kcols]
            vc = v_ref[rows, vcols]
            state = state_ref[h]
            scores = lax.dot_general(qc, kc, (((1,), (1,)), ((), ())),
                                     preferred_element_type=F32) * decay_ref[h]
            o = (jnp.dot(scores.astype(BF16), vc, preferred_element_type=F32)
                 + jnp.dot(qc, state.astype(BF16), preferred_element_type=F32) * q_decays[h])
            kd = (kc.astype(F32) * k_decays[h]).astype(BF16)
            state_ref[h] = state * math.exp(log_gs[h] * C) + lax.dot_general(
                kd, vc, (((0,), (0,)), ((), ())), preferred_element_type=F32)
            mu = jnp.mean(o, axis=-1, keepdims=True)
            var = jnp.mean(jnp.square(o - mu), axis=-1, keepdims=True)
            on = (o - mu) * lax.rsqrt(var + EPS)
            a_ref[rows, vcols] = (gate_ref[rows, vcols].astype(F32) * on).astype(BF16)
        o_ref[rows, :] = x_ref[rows, :] + jnp.dot(a_ref[rows, :], wo_ref[...],
                                                  preferred_element_type=F32)


def _retention(x, q, k, xn, gate, wv, wo):
    B, S, D = x.shape
    kw, vw = q.shape[-1], gate.shape[-1]
    tc = RET_TILE
    return pl.pallas_call(
        _retention_kernel,
        grid=(B, S // tc),
        in_specs=[_rows(tc, D), _rows(tc, kw), _rows(tc, kw), _rows(tc, D), _rows(tc, vw),
                  _resident(wv.shape), _resident(wo.shape)],
        out_specs=_rows(tc, D),
        out_shape=jax.ShapeDtypeStruct((B, S, D), F32),
        scratch_shapes=[pltpu.VMEM((RET_HEADS, RET_KDIM, RET_VDIM), F32),
                        pltpu.VMEM((RET_HEADS, RET_CHUNK, RET_CHUNK), F32),
                        pltpu.VMEM((tc, vw), BF16), pltpu.VMEM((tc, vw), BF16)],
        compiler_params=_params("arbitrary", "arbitrary"),
        name="retention",
    )(x, q, k, xn, gate, wv, wo)


def _head_sum_matrix():
    g = jnp.arange(A_WIDTH // 2) // A_HEAD_DIM
    return (g[:, None] == g[None, :]).astype(BF16)


def _head_expand_matrix():
    head_of_lane = jnp.arange(A_WIDTH) // A_HEAD_DIM
    return (jnp.arange(LANES)[:, None] == head_of_lane[None, :]).astype(BF16)


def kernel(x, even_norm, even_w_in, even_q_gain, even_k_gain, even_sconv_w, even_w_out, odd_norm,
           ret_wq, ret_wk, ret_wv, ret_wg, ret_gn_gain, ret_wo, ffn_norm, ffn_w_up, ffn_conv_w,
           ffn_w_down):
    B, S, D = x.shape
    depth = ffn_norm.shape[0]
    bf = lambda w: w.astype(BF16)
    row = lambda v: v.reshape(1, -1).astype(F32)
    hsum = _head_sum_matrix()
    expand = _head_expand_matrix()
    cos = sin = None

    for l in range(depth):
        if l % 2 == 0:
            e = l // 2
            n_qkv = 3 * A_WIDTH
            qs, ks, vs = _inproj(x, row(even_norm[e]), bf(even_w_in[e][:, :n_qkv]),
                                 row(jnp.tile(even_q_gain[e], A_HEADS) * _Q_SCALE),
                                 row(jnp.tile(even_k_gain[e], A_HEADS)), hsum)
            os, lses = zip(*[_dilated_attention_one(q, k, v, d)
                             for q, k, v, d in zip(qs, ks, vs, DILATIONS)])
            x = _outproj(x, os, lses, row(even_norm[e]), bf(even_w_in[e][:, n_qkv:]),
                         even_sconv_w[e], expand, bf(even_w_out[e]))
        else:
            o = l // 2
            if cos is None:
                cos, sin = _rope_table(S)
            q, k, xn, gate = _retproj(x, row(odd_norm[o]), bf(ret_wq[o]), bf(ret_wk[o]),
                                      bf(ret_wg[o]), cos, sin)
            x = _retention(x, q, k, xn, gate, bf(ret_wv[o]),
                           bf(ret_gn_gain[o].reshape(-1, 1) * ret_wo[o]))
        x = _ffn(x, row(ffn_norm[l]), bf(ffn_w_up[l]), ffn_conv_w[l], bf(ffn_w_down[l]))
    return x
```
